```python
import numpy as np
import jax
import jax.numpy as jnp
from jax import lax

D_MODEL = 2048
BATCH = 4
SEQ = 2048
DEPTH = 2

EPS = 1e-6
F32 = jnp.float32

MIX_WIDTH = D_MODEL
NSA_HEAD_DIM = 128
NSA_WIDTH = MIX_WIDTH // 2
NSA_HEADS = NSA_WIDTH // NSA_HEAD_DIM
NSA_KV_GROUPS = 2
NSA_HPG = NSA_HEADS // NSA_KV_GROUPS
KV_WIDTH = NSA_KV_GROUPS * NSA_HEAD_DIM
CMP_BLOCK = 32
CMP_STRIDE = 16
CMP_HIDDEN = 256
SLC_BLOCK = 64
SLC_TOPN = 16
N_LOCAL_FORCED = 2
WINDOW = 512
WIN_Q_BLOCK = 128
SLC_Q_BLOCK = 64
ROT_DIM = NSA_HEAD_DIM // 4
ROPE_THETA = 500000.0
MAX_POS_OFFSET = 1024
HGRN_WIDTH = MIX_WIDTH - NSA_WIDTH
HGRN_HEADS = 8
HGRN_VAL_DIM = HGRN_WIDTH // HGRN_HEADS
HGRN_KEY_DIM = 128
HGRN_FDIM = HGRN_HEADS * HGRN_KEY_DIM
HGRN_CHUNK = 64
IN_SPLITS = (NSA_WIDTH, KV_WIDTH, KV_WIDTH, KV_WIDTH, KV_WIDTH, KV_WIDTH, KV_WIDTH, 3 * NSA_HEADS, HGRN_FDIM, HGRN_FDIM, HGRN_WIDTH, HGRN_WIDTH)
IN_WIDTH = sum(IN_SPLITS)
PEER_HEADS = 8
PEER_N_KEYS = 128
PEER_N_EXPERTS = PEER_N_KEYS * PEER_N_KEYS
PEER_TOPK = 16
PEER_QUERY_DIM = 256
PEER_TOKEN_CHUNK = 128

kernel_name = 'hymba_nsa_hgrn2_peer_adaln_trunk'


def rms_norm(x, gain):
    xf = x.astype(F32)
    y = xf * lax.rsqrt(jnp.mean(xf * xf, axis=-1, keepdims=True) + EPS)
    return (y * gain.astype(F32)).astype(x.dtype)


def ada_modulate(x, gain, shift, scale):
    return rms_norm(x, gain) * (1.0 + scale[:, None, :]) + shift[:, None, :]


def rope_tables(positions):
    inv = ROPE_THETA ** (-jnp.arange(0, ROT_DIM, 2, dtype=F32) / ROT_DIM)
    ang = positions.astype(F32)[:, :, None] * inv
    return jnp.cos(ang)[:, :, None, :], jnp.sin(ang)[:, :, None, :]


def partial_rope(t, cos, sin):
    half = ROT_DIM // 2
    c = cos.astype(t.dtype)
    s = sin.astype(t.dtype)
    t1 = t[..., :half]
    t2 = t[..., half:ROT_DIM]
    return jnp.concatenate([t1 * c - t2 * s, t2 * c + t1 * s, t[..., ROT_DIM:]], axis=-1)


def compress_blocks(t, pos_emb, w1, b1, w2, b2):
    B, S, G, d = t.shape
    n_cmp = (S - CMP_BLOCK) // CMP_STRIDE + 1
    tok = np.arange(n_cmp)[:, None] * CMP_STRIDE + np.arange(CMP_BLOCK)[None, :]
    blk = t[:, tok] + pos_emb[None, None, :, None, :]
    flat = blk.transpose(0, 1, 3, 2, 4).reshape(B, n_cmp, G, CMP_BLOCK * d)
    return jax.nn.gelu(flat @ w1 + b1) @ w2 + b2


def cmp_to_slc_weights(n_cmp, n_slc):
    cs = np.arange(n_cmp) * CMP_STRIDE
    ce = cs + CMP_BLOCK
    ss = np.arange(n_slc) * SLC_BLOCK
    se = ss + SLC_BLOCK
    ov = np.clip(np.minimum(ce[:, None], se[None, :]) - np.maximum(cs[:, None], ss[None, :]), 0, None)
    return (ov / CMP_BLOCK).astype(np.float32)


def nsa_compressed(q, kc, vc):
    B, S, H, d = q.shape
    n_cmp = kc.shape[1]
    qg = q.reshape(B, S, NSA_KV_GROUPS, NSA_HPG, d)
    s = jnp.einsum('bsghd,bngd->bghsn', qg, kc, preferred_element_type=F32) * (d ** -0.5)
    valid = (np.arange(n_cmp) * CMP_STRIDE + CMP_BLOCK - 1)[None, :] <= np.arange(S)[:, None]
    s = jnp.where(valid, s, -jnp.inf)
    m = jnp.max(s, axis=-1, keepdims=True)
    m = jnp.where(jnp.isfinite(m), m, 0.0)
    e = jnp.where(valid, jnp.exp(s - m), 0.0)
    p = e / jnp.maximum(jnp.sum(e, axis=-1, keepdims=True), 1e-30)
    o = jnp.einsum('bghsn,bngd->bsghd', p.astype(vc.dtype), vc).reshape(B, S, H, d)
    return o, p


def nsa_select(p_cmp, S):
    n_cmp = p_cmp.shape[-1]
    n_slc = S // SLC_BLOCK
    w = jnp.asarray(cmp_to_slc_weights(n_cmp, n_slc))
    imp = jnp.einsum('bghsn,nj->bgsj', p_cmp, w)
    t = np.arange(S)[:, None]
    j = np.arange(n_slc)[None, :]
    causal = j * SLC_BLOCK <= t
    back = t // SLC_BLOCK - j
    forced = (j == 0) | ((back >= 0) & (back < N_LOCAL_FORCED))
    score = jnp.where(forced, jnp.inf, jnp.where(causal, imp, -jnp.inf))
    _, idx = lax.top_k(score, min(SLC_TOPN, n_slc))
    return idx


def nsa_selected(qr, ks, vs, idx):
    B, S, H, d = qr.shape
    G, HPG, L, C = NSA_KV_GROUPS, NSA_HPG, SLC_BLOCK, SLC_Q_BLOCK
    n_slc = S // L
    n_sel = idx.shape[-1]
    nqc = S // C
    k_blk = ks.reshape(B, n_slc, L, G, d).transpose(0, 3, 1, 2, 4)
    v_blk = vs.reshape(B, n_slc, L, G, d).transpose(0, 3, 1, 2, 4)
    q_ch = qr.reshape(B, nqc, C, G, HPG, d).transpose(1, 0, 3, 4, 2, 5)
    i_ch = idx.reshape(B, G, nqc, C, n_sel).transpose(2, 0, 1, 3, 4)
    starts = jnp.arange(nqc, dtype=jnp.int32) * C
    bi = jnp.arange(B)[:, None, None, None]
    gi = jnp.arange(G)[None, :, None, None]
    offs = jnp.arange(L, dtype=jnp.int32)
    qoff = jnp.arange(C, dtype=jnp.int32)

    def block(args):
        qc, ic, st = args
        kg = k_blk[bi, gi, ic]
        vg = v_blk[bi, gi, ic]
        s = jnp.einsum('bghqd,bgqnld->bghqnl', qc, kg, preferred_element_type=F32) * (d ** -0.5)
        kpos = ic[..., None] * L + offs
        ok = kpos <= (st + qoff)[None, None, :, None, None]
        s = jnp.where(ok[:, :, None], s, -jnp.inf)
        p = jax.nn.softmax(s.reshape(B, G, HPG, C, n_sel * L), axis=-1).reshape(s.shape)
        return jnp.einsum('bghqnl,bgqnld->bghqd', p.astype(vg.dtype), vg)

    o = lax.map(block, (q_ch, i_ch, starts))
    return o.transpose(1, 0, 4, 2, 3, 5).reshape(B, S, H, d)


def nsa_window(qr, kw, vw):
    B, S, H, d = qr.shape
    G, HPG, QB = NSA_KV_GROUPS, NSA_HPG, WIN_Q_BLOCK
    nb = S // QB
    wb = WINDOW // QB
    KW = (wb + 1) * QB
    pad = ((0, 0), (wb * QB, 0), (0, 0), (0, 0))
    kp = jnp.pad(kw, pad).reshape(B, nb + wb, QB, G, d)
    vp = jnp.pad(vw, pad).reshape(B, nb + wb, QB, G, d)
    band = np.arange(nb)[:, None] + np.arange(wb + 1)[None, :]
    kb = kp[:, band].reshape(B, nb, KW, G, d)
    vb = vp[:, band].reshape(B, nb, KW, G, d)
    qb = qr.reshape(B, nb, QB, G, HPG, d)
    s = jnp.einsum('bnqghd,bnkgd->bnghqk', qb, kb, preferred_element_type=F32) * (d ** -0.5)
    qpos = np.arange(nb)[:, None] * QB + np.arange(QB)[None, :]
    kpos = (np.arange(nb)[:, None] - wb) * QB + np.arange(KW)[None, :]
    diff = qpos[:, :, None] - kpos[:, None, :]
    ok = (diff >= 0) & (diff < WINDOW) & (kpos[:, None, :] >= 0)
    s = jnp.where(ok[None, :, None, None], s, -jnp.inf)
    p = jax.nn.softmax(s, axis=-1)
    return jnp.einsum('bnghqk,bnkgd->bnqghd', p.astype(vb.dtype), vb).reshape(B, S, H, d)


def nsa_mixer(q, k_cmp, v_cmp, k_slc, v_slc, k_win, v_win, gate_logits, cos, sin,
              ck_pos, ck_w1, ck_b1, ck_w2, ck_b2, cv_pos, cv_w1, cv_b1, cv_w2, cv_b2):
    B, S, _ = q.shape
    H, G, d = NSA_HEADS, NSA_KV_GROUPS, NSA_HEAD_DIM
    q = q.reshape(B, S, H, d)
    kv = lambda t: t.reshape(B, S, G, d)
    kc = compress_blocks(kv(k_cmp), ck_pos, ck_w1, ck_b1, ck_w2, ck_b2)
    vc = compress_blocks(kv(v_cmp), cv_pos, cv_w1, cv_b1, cv_w2, cv_b2)
    o_cmp, p_cmp = nsa_compressed(q, kc, vc)
    idx = nsa_select(p_cmp, S)
    qr = partial_rope(q, cos, sin)
    o_slc = nsa_selected(qr, partial_rope(kv(k_slc), cos, sin), kv(v_slc), idx)
    o_win = nsa_window(qr, partial_rope(kv(k_win), cos, sin), kv(v_win))
    gate = jax.nn.sigmoid(gate_logits.reshape(B, S, H, 3))
    o = gate[..., 0:1] * o_cmp + gate[..., 1:2] * o_slc + gate[..., 2:3] * o_win
    return o.reshape(B, S, H * d)


def hgrn2_mixer(q, f, i, g, lb, norm_gain):
    B, S, _ = q.shape
    H, dk, dv, C = HGRN_HEADS, HGRN_KEY_DIM, HGRN_VAL_DIM, HGRN_CHUNK
    nc = S // C

    def heads(t, dd):
        return t.reshape(B, nc, C, H, dd).transpose(1, 0, 3, 2, 4).astype(F32)

    qh = heads(jax.nn.silu(q), dk)
    fh = heads(f, dk)
    vh = heads(i, dv)
    lbh = lb.astype(F32).reshape(H, 1, dk)
    log_f = jnp.logaddexp(jnp.log(lbh), jnp.log1p(-lbh) + jax.nn.log_sigmoid(fh))
    kh = -jnp.expm1(log_f)
    causal = np.tril(np.ones((C, C), dtype=bool))

    def step(state, xs):
        qc, kc, vc, lc = xs
        A = jnp.cumsum(lc, axis=-2)
        inter = jnp.einsum('bhtd,bhde->bhte', qc * jnp.exp(A), state)
        rel = A[:, :, :, None, :] - A[:, :, None, :, :]
        decay = jnp.exp(jnp.where(causal[:, :, None], rel, -jnp.inf))
        attn = jnp.einsum('bhtd,bhsd,bhtsd->bhts', qc, kc, decay)
        intra = jnp.einsum('bhts,bhse->bhte', attn, vc)
        A_end = A[:, :, -1:, :]
        state = jnp.exp(A_end[:, :, 0, :])[..., None] * state + jnp.einsum('bhsd,bhse->bhde', kc * jnp.exp(A_end - A), vc)
        return state, inter + intra

    s0 = jnp.zeros((B, H, dk, dv), F32)
    _, o = lax.scan(step, s0, (qh, kh, vh, log_f))
    o = o.transpose(1, 0, 3, 2, 4).reshape(B, S, H, dv)
    o = o * lax.rsqrt(jnp.mean(o * o, axis=-1, keepdims=True) + EPS) * norm_gain.astype(F32).reshape(H, dv)
    o = o * jax.nn.sigmoid(g.astype(F32).reshape(B, S, H, dv))
    return o.reshape(B, S, H * dv).astype(q.dtype)


def peer_ffn(h, wq, sub_keys, u, v):
    B, S, D = h.shape
    T = B * S
    PH, K, NK = PEER_HEADS, PEER_TOPK, PEER_N_KEYS
    xt = h.reshape(T, D)
    q = (xt @ wq).reshape(T, PH, 2, PEER_QUERY_DIM // 2)
    s = jnp.einsum('thpd,hpkd->thpk', q, sub_keys, preferred_element_type=F32)
    s_top, i_top = lax.top_k(s, K)
    cand_s = (s_top[:, :, 0, :, None] + s_top[:, :, 1, None, :]).reshape(T, PH, K * K)
    cand_i = (i_top[:, :, 0, :, None] * NK + i_top[:, :, 1, None, :]).reshape(T, PH, K * K)
    best_s, best_pos = lax.top_k(cand_s, K)
    experts = jnp.take_along_axis(cand_i, best_pos, axis=-1)
    gates = jax.nn.softmax(best_s, axis=-1).astype(h.dtype)
    n_ch = T // PEER_TOKEN_CHUNK

    def chunk(args):
        xc, ec, gc = args
        act = jax.nn.gelu(jnp.einsum('td,thkd->thk', xc, u[ec]))
        return jnp.einsum('thk,thkd->td', gc * act, v[ec])

    out = lax.map(chunk, (xt.reshape(n_ch, PEER_TOKEN_CHUNK, D),
                          experts.reshape(n_ch, PEER_TOKEN_CHUNK, PH, K),
                          gates.reshape(n_ch, PEER_TOKEN_CHUNK, PH, K)))
    return out.reshape(B, S, D)


def setup_inputs(seed: int = 0) -> dict:
    key = jax.random.key(seed)
    ks = jax.random.split(key, 32)
    L, D = DEPTH, D_MODEL
    nrm = lambda k, shape, scale: jax.random.normal(k, shape, F32) * scale
    lk = CMP_BLOCK * NSA_HEAD_DIM
    x = nrm(ks[0], (BATCH, SEQ, D), 1.0)
    c = nrm(ks[1], (BATCH, D), 1.0)
    positions = (jax.random.randint(ks[2], (BATCH, 1), 0, MAX_POS_OFFSET, dtype=jnp.int32)
                 + jnp.arange(SEQ, dtype=jnp.int32)[None, :]).astype(jnp.int32)
    return {
        'x': x,
        'c': c,
        'positions': positions,
        'norm_mix': 1.0 + nrm(ks[3], (L, D), 0.02),
        'norm_ffn': 1.0 + nrm(ks[4], (L, D), 0.02),
        'w_ada': nrm(ks[5], (L, D, 6 * D), 0.5 * D ** -0.5),
        'b_ada': nrm(ks[6], (L, 6 * D), 0.01),
        'w_in': nrm(ks[7], (L, D, IN_WIDTH), D ** -0.5),
        'w_out': nrm(ks[8], (L, MIX_WIDTH, D), MIX_WIDTH ** -0.5),
        'cmp_k_pos': nrm(ks[9], (L, CMP_BLOCK, NSA_HEAD_DIM), 0.1),
        'cmp_k_w1': nrm(ks[10], (L, lk, CMP_HIDDEN), lk ** -0.5),
        'cmp_k_b1': nrm(ks[11], (L, CMP_HIDDEN), 0.01),
        'cmp_k_w2': nrm(ks[12], (L, CMP_HIDDEN, NSA_HEAD_DIM), CMP_HIDDEN ** -0.5),
        'cmp_k_b2': nrm(ks[13], (L, NSA_HEAD_DIM), 0.01),
        'cmp_v_pos': nrm(ks[14], (L, CMP_BLOCK, NSA_HEAD_DIM), 0.1),
        'cmp_v_w1': nrm(ks[15], (L, lk, CMP_HIDDEN), lk ** -0.5),
        'cmp_v_b1': nrm(ks[16], (L, CMP_HIDDEN), 0.01),
        'cmp_v_w2': nrm(ks[17], (L, CMP_HIDDEN, NSA_HEAD_DIM), CMP_HIDDEN ** -0.5),
        'cmp_v_b2': nrm(ks[18], (L, NSA_HEAD_DIM), 0.01),
        'hgrn_norm': 1.0 + nrm(ks[19], (L, HGRN_WIDTH), 0.02),
        'hgrn_lb': nrm(ks[20], (L, HGRN_FDIM), 0.5),
        'peer_wq': nrm(ks[21], (L, D, PEER_HEADS * PEER_QUERY_DIM), D ** -0.5),
        'peer_keys': nrm(ks[22], (L, PEER_HEADS, 2, PEER_N_KEYS, PEER_QUERY_DIM // 2), (PEER_QUERY_DIM // 2) ** -0.5),
        'peer_u': nrm(ks[23], (L, PEER_N_EXPERTS, D), D ** -0.5),
        'peer_v': nrm(ks[24], (L, PEER_N_EXPERTS, D), 1.0),
        'final_norm': 1.0 + nrm(ks[25], (D,), 0.02),
    }


def reference(x, c, positions, norm_mix, norm_ffn, w_ada, b_ada, w_in, w_out,
              cmp_k_pos, cmp_k_w1, cmp_k_b1, cmp_k_w2, cmp_k_b2,
              cmp_v_pos, cmp_v_w1, cmp_v_b1, cmp_v_w2, cmp_v_b2,
              hgrn_norm, hgrn_lb, peer_wq, peer_keys, peer_u, peer_v, final_norm):
    cos, sin = rope_tables(positions)
    c_act = jax.nn.silu(c)
    lb_all = jnp.cumsum(jax.nn.softmax(hgrn_lb.astype(F32), axis=0), axis=0)
    lb_all = lb_all - lb_all[0:1]
    split_at = [int(v) for v in np.cumsum(IN_SPLITS)[:-1]]
    for l in range(DEPTH):
        mod = c_act @ w_ada[l] + b_ada[l]
        sh1, sc1, g1, sh2, sc2, g2 = jnp.split(mod, 6, axis=-1)
        h = ada_modulate(x, norm_mix[l], sh1, sc1)
        q, kc, vc, ks, vs, kw, vw, gl, hq, hf, hi, hg = jnp.split(h @ w_in[l], split_at, axis=-1)
        o_nsa = nsa_mixer(q, kc, vc, ks, vs, kw, vw, gl, cos, sin,
                          cmp_k_pos[l], cmp_k_w1[l], cmp_k_b1[l], cmp_k_w2[l], cmp_k_b2[l],
                          cmp_v_pos[l], cmp_v_w1[l], cmp_v_b1[l], cmp_v_w2[l], cmp_v_b2[l])
        o_hgrn = hgrn2_mixer(hq, hf, hi, hg, lb_all[l], hgrn_norm[l])
        mix = jnp.concatenate([o_nsa, o_hgrn], axis=-1) @ w_out[l]
        x = x + g1[:, None, :] * mix
        h = ada_modulate(x, norm_ffn[l], sh2, sc2)
        x = x + g2[:, None, :] * peer_ffn(h, peer_wq[l], peer_keys[l], peer_u[l], peer_v[l])
    return rms_norm(x, final_norm)
```

```python
import functools

import numpy as np
import jax
import jax.numpy as jnp
from jax import lax
from jax.experimental import pallas as pl
from jax.experimental.pallas import tpu as pltpu

F32 = jnp.float32
BF16 = jnp.bfloat16

D_MODEL = 2048
BATCH = 4
SEQ = 2048
TOKENS = BATCH * SEQ
DEPTH = 2
EPS = 1e-6

HEAD_DIM = 128
NSA_HEADS = 8
NSA_GROUPS = 2
NSA_HPG = NSA_HEADS // NSA_GROUPS
CMP_BLOCK = 32
CMP_STRIDE = 16
CMP_HIDDEN = 256
N_CMP = (SEQ - CMP_BLOCK) // CMP_STRIDE + 1
N_CMP_PAD = 128
SLC_BLOCK = 64
N_SLC = SEQ // SLC_BLOCK
SLC_TOPN = 16
N_LOCAL_FORCED = 2
WINDOW = 512
ROT_HALF = 16
ROPE_THETA = 500000.0
Q_TILE = 128
WIN_KEYS = WINDOW + Q_TILE

HGRN_HEADS = 8
HGRN_CHUNK = 64
HGRN_SUB = 16

PEER_HEADS = 8
PEER_KEYS = 128
PEER_EXPERTS = PEER_KEYS * PEER_KEYS
PEER_TOPK = 16

IN_WIDTH_PACKED = 6912
COL_Q, COL_KC, COL_VC, COL_KS, COL_VS, COL_KW, COL_VW = 0, 8, 10, 12, 14, 16, 18
COL_HQ, COL_HF, COL_HI, COL_HG, COL_GL = 20, 28, 36, 44, 52

VMEM_LIMIT = 56 * 1024 * 1024


def _params(sem):
    return pltpu.CompilerParams(dimension_semantics=sem, vmem_limit_bytes=VMEM_LIMIT)


def _ada_kernel(c_ref, w_ref, b_ref, o_ref):
    c = c_ref[...]
    o_ref[0] = jnp.dot(c * jax.nn.sigmoid(c), w_ref[0], preferred_element_type=F32) + b_ref[0]


def ada_mod(c8, w_ada, b_ada):
    tn = 1024
    n = 6 * D_MODEL
    return pl.pallas_call(
        _ada_kernel,
        grid=(DEPTH, n // tn),
        in_specs=[
            pl.BlockSpec((8, D_MODEL), lambda l, j: (0, 0)),
            pl.BlockSpec((1, D_MODEL, tn), lambda l, j: (l, 0, j)),
            pl.BlockSpec((1, 1, tn), lambda l, j: (l, 0, j)),
        ],
        out_specs=pl.BlockSpec((1, 8, tn), lambda l, j: (l, 0, j)),
        out_shape=jax.ShapeDtypeStruct((DEPTH, 8, n), F32),
        compiler_params=_params(("parallel", "parallel")),
        name="ada_mod",
    )(c8, w_ada, b_ada.reshape(DEPTH, 1, n))


def _mod_spec(layer, k, rows_per_batch):
    return pl.BlockSpec((1, 1, D_MODEL), lambda i: ((layer * 8 + i // rows_per_batch) * 6 + k, 0, 0))


def _rms_mod(x, gain, scale, shift):
    y = x * lax.rsqrt(jnp.mean(x * x, axis=-1, keepdims=True) + EPS) * gain
    return y * (1.0 + scale) + shift


def _adaln_kernel(x_ref, gain_ref, sh_ref, sc_ref, h_ref):
    h_ref[...] = _rms_mod(x_ref[...], gain_ref[...], sc_ref[0], sh_ref[0]).astype(BF16)


def adaln(x, gain, mod, layer):
    tm = 256
    return pl.pallas_call(
        _adaln_kernel,
        grid=(TOKENS // tm,),
        in_specs=[
            pl.BlockSpec((tm, D_MODEL), lambda i: (i, 0)),
            pl.BlockSpec((1, D_MODEL), lambda i: (0, 0)),
            _mod_spec(layer, 0, SEQ // tm),
            _mod_spec(layer, 1, SEQ // tm),
        ],
        out_specs=pl.BlockSpec((tm, D_MODEL), lambda i: (i, 0)),
        out_shape=jax.ShapeDtypeStruct((TOKENS, D_MODEL), BF16),
        compiler_params=_params(("parallel",)),
        name="adaln",
    )(x, gain.reshape(1, D_MODEL), mod, mod)


def _matmul_kernel(a_ref, b_ref, o_ref):
    o_ref[...] = jnp.dot(a_ref[...], b_ref[...], preferred_element_type=F32)


def in_proj(h, w):
    tm, tn = 1024, 768
    return pl.pallas_call(
        _matmul_kernel,
        grid=(TOKENS // tm, IN_WIDTH_PACKED // tn),
        in_specs=[
            pl.BlockSpec((tm, D_MODEL), lambda i, j: (i, 0)),
            pl.BlockSpec((D_MODEL, tn), lambda i, j: (0, j)),
        ],
        out_specs=pl.BlockSpec((tm, tn), lambda i, j: (i, j)),
        out_shape=jax.ShapeDtypeStruct((TOKENS, IN_WIDTH_PACKED), F32),
        compiler_params=_params(("parallel", "parallel")),
        name="in_proj",
    )(h, w)


def pack_w_in(w):
    gl = w[:, 2560:2584]
    z = jnp.zeros((D_MODEL, 116), w.dtype)
    return jnp.concatenate([w[:, :2560], w[:, 2584:], gl[:, :12], z, gl[:, 12:], z], axis=1).astype(BF16)


def _compress_kernel(t_ref, pos_ref, w1_ref, b1_ref, w2_ref, b2_ref, o_ref, pad_ref):
    pad_ref[0:SEQ, :] = t_ref[...]
    pad_ref[SEQ:SEQ + CMP_BLOCK, :] = jnp.zeros((CMP_BLOCK, HEAD_DIM), F32)
    acc = jnp.zeros((N_CMP_PAD, CMP_HIDDEN), F32)
    for l in range(CMP_BLOCK):
        tl = pad_ref[pl.ds(l, N_CMP_PAD, stride=CMP_STRIDE), :] + pos_ref[0, l:l + 1, :]
        acc = acc + jnp.dot(tl.astype(BF16), w1_ref[0, l * HEAD_DIM:(l + 1) * HEAD_DIM, :],
                            preferred_element_type=F32)
    hid = jax.nn.gelu(acc + b1_ref[0])
    o_ref[0, 0] = jnp.dot(hid.astype(BF16), w2_ref[0], preferred_element_type=F32) + b2_ref[0]


def compress(proj, pos, w1, b1, w2, b2):
    return pl.pallas_call(
        _compress_kernel,
        grid=(2, BATCH, NSA_GROUPS),
        in_specs=[
            pl.BlockSpec((SEQ, HEAD_DIM), lambda kv, b, g: (b, COL_KC + 2 * kv + g)),
            pl.BlockSpec((1, CMP_BLOCK, HEAD_DIM), lambda kv, b, g: (kv, 0, 0)),
            pl.BlockSpec((1, CMP_BLOCK * HEAD_DIM, CMP_HIDDEN), lambda kv, b, g: (kv, 0, 0)),
            pl.BlockSpec((1, 1, CMP_HIDDEN), lambda kv, b, g: (kv, 0, 0)),
            pl.BlockSpec((1, CMP_HIDDEN, HEAD_DIM), lambda kv, b, g: (kv, 0, 0)),
            pl.BlockSpec((1, 1, HEAD_DIM), lambda kv, b, g: (kv, 0, 0)),
        ],
        out_specs=pl.BlockSpec((1, 1, N_CMP_PAD, HEAD_DIM), lambda kv, b, g: (kv, b * NSA_GROUPS + g, 0, 0)),
        out_shape=jax.ShapeDtypeStruct((2, BATCH * NSA_GROUPS, N_CMP_PAD, HEAD_DIM), F32),
        scratch_shapes=[pltpu.VMEM((SEQ + CMP_BLOCK, HEAD_DIM), F32)],
        compiler_params=_params(("arbitrary", "arbitrary", "arbitrary")),
        name="nsa_compress",
    )(proj, pos, w1, b1, w2, b2)


def _rope(t, cs, sn):
    lane = lax.broadcasted_iota(jnp.int32, t.shape, 1)
    swapped = jnp.where(lane < ROT_HALF, pltpu.roll(t, HEAD_DIM - ROT_HALF, axis=1), pltpu.roll(t, ROT_HALF, axis=1))
    return t * cs + swapped * sn


def _dot_nt(a, b):
    return lax.dot_general(a, b, (((1,), (1,)), ((), ())), preferred_element_type=F32)


def _nsa_kernel(q_ref, kc_ref, vc_ref, ks_ref, vs_ref, kw_ref, vw_ref, gl_ref, cs_ref, sn_ref, wov_ref, exp_ref,
                o_ref, ksr, vsb, kwr, vwb):
    scale = HEAD_DIM ** -0.5
    neg_inf = -jnp.inf
    ksr[...] = _rope(ks_ref[...], cs_ref[...], sn_ref[...]).astype(BF16)
    kwr[...] = _rope(kw_ref[...], cs_ref[...], sn_ref[...]).astype(BF16)
    vsb[...] = vs_ref[...].astype(BF16)
    vwb[...] = vw_ref[...].astype(BF16)
    kcb = kc_ref[0, 0].astype(BF16)
    vcb = vc_ref[0, 0].astype(BF16)

    def tile(i, carry):
        t0 = pl.multiple_of(i * Q_TILE, Q_TILE)
        q4 = q_ref[pl.ds(t0, Q_TILE), :]
        cs = cs_ref[pl.ds(t0, Q_TILE), :]
        sn = sn_ref[pl.ds(t0, Q_TILE), :]
        gate = jax.nn.sigmoid(gl_ref[pl.ds(t0, Q_TILE), :])
        qh = [q4[:, h * HEAD_DIM:(h + 1) * HEAD_DIM] for h in range(NSA_HPG)]

        qs = jnp.concatenate(qh, axis=0).astype(BF16)
        sc = _dot_nt(qs, kcb) * scale
        rows = lax.broadcasted_iota(jnp.int32, sc.shape, 0)
        ncol = lax.broadcasted_iota(jnp.int32, sc.shape, 1)
        tok = t0 + (rows & (Q_TILE - 1))
        valid = ncol * CMP_STRIDE + (CMP_BLOCK - 1) <= tok
        s = jnp.where(valid, sc, neg_inf)
        m = jnp.max(s, axis=-1, keepdims=True)
        m = jnp.where(m == neg_inf, 0.0, m)
        e = jnp.where(valid, jnp.exp(s - m), 0.0)
        p = e / jnp.maximum(jnp.sum(e, axis=-1, keepdims=True), 1e-30)
        o_cmp = jnp.dot(p.astype(BF16), vcb, preferred_element_type=F32)

        p4 = p[0:Q_TILE] + p[Q_TILE:2 * Q_TILE] + p[2 * Q_TILE:3 * Q_TILE] + p[3 * Q_TILE:4 * Q_TILE]
        imp = jnp.dot(p4, wov_ref[...], preferred_element_type=F32)
        jj = lax.broadcasted_iota(jnp.int32, imp.shape, 1)
        tt = t0 + lax.broadcasted_iota(jnp.int32, imp.shape, 0)
        back = (tt >> 6) - jj
        forced = (jj == 0) | (back.astype(jnp.uint32) < N_LOCAL_FORCED)
        score = jnp.where(forced, jnp.inf, jnp.where(jj * SLC_BLOCK <= tt, imp, neg_inf))
        rank = jnp.zeros(imp.shape, F32)
        for jp in range(N_SLC):
            col = score[:, jp:jp + 1]
            later = jnp.where(jj > jp, 1.0, 0.0)
            rank = rank + jnp.where(col > score, 1.0, jnp.where(col == score, later, 0.0))
        sel = jnp.where(rank < SLC_TOPN, 1.0, 0.0).astype(BF16)
        selk = jnp.dot(sel, exp_ref[...], preferred_element_type=F32)
        kpos = lax.broadcasted_iota(jnp.int32, selk.shape, 1)
        tk = t0 + lax.broadcasted_iota(jnp.int32, selk.shape, 0)
        ok_sel = jnp.where(kpos <= tk, selk, 0.0) > 0.5

        ws = pl.multiple_of(jnp.clip(t0 - WINDOW, 0, SEQ - WIN_KEYS), Q_TILE)
        kwt = kwr[pl.ds(ws, WIN_KEYS), :]
        vwt = vwb[pl.ds(ws, WIN_KEYS), :]
        wk = ws + lax.broadcasted_iota(jnp.int32, (Q_TILE, WIN_KEYS), 1)
        wt = t0 + lax.broadcasted_iota(jnp.int32, (Q_TILE, WIN_KEYS), 0)
        ok_win = (wt - wk).astype(jnp.uint32) < WINDOW

        outs = []
        for h in range(NSA_HPG):
            qr = _rope(qh[h], cs, sn).astype(BF16)
            s1 = jnp.where(ok_sel, _dot_nt(qr, ksr[...]) * scale, neg_inf)
            e1 = jnp.exp(s1 - jnp.max(s1, axis=-1, keepdims=True))
            o_slc = jnp.dot(e1.astype(BF16), vsb[...], preferred_element_type=F32) / jnp.sum(e1, axis=-1, keepdims=True)
            s2 = jnp.where(ok_win, _dot_nt(qr, kwt) * scale, neg_inf)
            e2 = jnp.exp(s2 - jnp.max(s2, axis=-1, keepdims=True))
            o_win = jnp.dot(e2.astype(BF16), vwt, preferred_element_type=F32) / jnp.sum(e2, axis=-1, keepdims=True)
            oc = o_cmp[h * Q_TILE:(h + 1) * Q_TILE]
            outs.append(gate[:, 3 * h:3 * h + 1] * oc + gate[:, 3 * h + 1:3 * h + 2] * o_slc
                        + gate[:, 3 * h + 2:3 * h + 3] * o_win)
        o_ref[pl.ds(t0, Q_TILE), :] = jnp.concatenate(outs, axis=1).astype(BF16)
        return carry

    lax.fori_loop(0, SEQ // Q_TILE, tile, 0)


def _cmp_to_slc_weights():
    cs = np.arange(N_CMP) * CMP_STRIDE
    ce = cs + CMP_BLOCK
    ss = np.arange(N_SLC) * SLC_BLOCK
    se = ss + SLC_BLOCK
    ov = np.clip(np.minimum(ce[:, None], se[None, :]) - np.maximum(cs[:, None], ss[None, :]), 0, None)
    w = np.zeros((N_CMP_PAD, HEAD_DIM), np.float32)
    w[:N_CMP, :N_SLC] = ov / CMP_BLOCK
    return w


def _slc_expand():
    e = np.zeros((HEAD_DIM, SEQ), np.float32)
    e[np.arange(SEQ) // SLC_BLOCK, np.arange(SEQ)] = 1.0
    return e


def nsa_attention(proj, cmp_kv, rope_cs, rope_sn):
    gw = NSA_HPG * HEAD_DIM
    col = lambda c: pl.BlockSpec((SEQ, HEAD_DIM), lambda b, g: (b, c + g))
    return pl.pallas_call(
        _nsa_kernel,
        grid=(BATCH, NSA_GROUPS),
        in_specs=[
            pl.BlockSpec((SEQ, gw), lambda b, g: (b, g)),
            pl.BlockSpec((1, 1, N_CMP_PAD, HEAD_DIM), lambda b, g: (0, b * NSA_GROUPS + g, 0, 0)),
            pl.BlockSpec((1, 1, N_CMP_PAD, HEAD_DIM), lambda b, g: (1, b * NSA_GROUPS + g, 0, 0)),
            col(COL_KS), col(COL_VS), col(COL_KW), col(COL_VW), col(COL_GL),
            pl.BlockSpec((SEQ, HEAD_DIM), lambda b, g: (b, 0)),
            pl.BlockSpec((SEQ, HEAD_DIM), lambda b, g: (b, 0)),
            pl.BlockSpec((N_CMP_PAD, HEAD_DIM), lambda b, g: (0, 0)),
            pl.BlockSpec((HEAD_DIM, SEQ), lambda b, g: (0, 0)),
        ],
        out_specs=pl.BlockSpec((SEQ, gw), lambda b, g: (b, g)),
        out_shape=jax.ShapeDtypeStruct((TOKENS, NSA_HEADS * HEAD_DIM), BF16),
        scratch_shapes=[pltpu.VMEM((SEQ, HEAD_DIM), BF16)] * 4,
        compiler_params=_params(("parallel", "parallel")),
        name="nsa_attention",
    )(proj, cmp_kv, cmp_kv, proj, proj, proj, proj, proj, rope_cs, rope_sn,
      jnp.asarray(_cmp_to_slc_weights()), jnp.asarray(_slc_expand(), dtype=BF16))


def _hgrn_kernel(q_ref, f_ref, i_ref, g_ref, lb_ref, gn_ref, tri_ref, o_ref):
    C, SUB = HGRN_CHUNK, HGRN_SUB
    lb = lb_ref[0]
    log_lb = jnp.log(lb)
    log_1m = jnp.log(1.0 - lb)
    gn = gn_ref[0]
    row_sub = lax.broadcasted_iota(jnp.int32, (SUB, HEAD_DIM), 0)

    def chunk(c, state_t):
        r0 = pl.multiple_of(c * C, C)
        qc = q_ref[pl.ds(r0, C), :]
        fc = f_ref[pl.ds(r0, C), :]
        vc = i_ref[pl.ds(r0, C), :]
        gc = g_ref[pl.ds(r0, C), :]
        qh = qc * jax.nn.sigmoid(qc)
        log_sig = jnp.minimum(fc, 0.0) - jnp.log(1.0 + jnp.exp(-jnp.abs(fc)))
        b = log_1m + log_sig
        lf = jnp.maximum(log_lb, b) + jnp.log(1.0 + jnp.exp(-jnp.abs(log_lb - b)))
        kh = (1.0 - lb) * jax.nn.sigmoid(-fc)
        a = jnp.dot(tri_ref[...], lf, preferred_element_type=F32)
        inter = _dot_nt(qh * jnp.exp(a), state_t)
        parts = []
        for blk in range(C // SUB):
            lo = blk * SUB
            ab, qb, kb, vb = a[lo:lo + SUB], qh[lo:lo + SUB], kh[lo:lo + SUB], vc[lo:lo + SUB]
            acc = inter[lo:lo + SUB]
            if blk > 0:
                a_ref_row = a[lo - 1:lo]
                qt = qb * jnp.exp(ab - a_ref_row)
                kt = kh[0:lo] * jnp.exp(a_ref_row - a[0:lo])
                acc = acc + jnp.dot(_dot_nt(qt, kt), vc[0:lo], preferred_element_type=F32)
            for s in range(SUB):
                dec = jnp.exp(jnp.where(row_sub >= s, ab - ab[s:s + 1], -jnp.inf))
                w = jnp.sum(qb * kb[s:s + 1] * dec, axis=-1, keepdims=True)
                acc = acc + w * vb[s:s + 1]
            parts.append(acc)
        o = jnp.concatenate(parts, axis=0)
        a_end = a[C - 1:C]
        kt_end = kh * jnp.exp(a_end - a)
        state_t = state_t * jnp.exp(a_end) + lax.dot_general(vc, kt_end, (((0,), (0,)), ((), ())),
                                                              preferred_element_type=F32)
        o = o * lax.rsqrt(jnp.mean(o * o, axis=-1, keepdims=True) + EPS) * gn
        o_ref[pl.ds(r0, C), :] = (o * jax.nn.sigmoid(gc)).astype(BF16)
        return state_t

    lax.fori_loop(0, SEQ // C, chunk, jnp.zeros((HEAD_DIM, HEAD_DIM), F32))


def hgrn(proj, lb, gain):
    col = lambda c: pl.BlockSpec((SEQ, HEAD_DIM), lambda b, h: (b, c + h))
    vec = pl.BlockSpec((1, 1, HEAD_DIM), lambda b, h: (h, 0, 0))
    tri = np.tril(np.ones((HGRN_CHUNK, HGRN_CHUNK), np.float32))
    return pl.pallas_call(
        _hgrn_kernel,
        grid=(BATCH, HGRN_HEADS),
        in_specs=[col(COL_HQ), col(COL_HF), col(COL_HI), col(COL_HG), vec, vec,
                  pl.BlockSpec((HGRN_CHUNK, HGRN_CHUNK), lambda b, h: (0, 0))],
        out_specs=pl.BlockSpec((SEQ, HEAD_DIM), lambda b, h: (b, h)),
        out_shape=jax.ShapeDtypeStruct((TOKENS, HGRN_HEADS * HEAD_DIM), BF16),
        compiler_params=_params(("parallel", "parallel")),
        name="hgrn2",
    )(proj, proj, proj, proj, lb.reshape(HGRN_HEADS, 1, HEAD_DIM), gain.reshape(HGRN_HEADS, 1, HEAD_DIM),
      jnp.asarray(tri))


def _out_proj_kernel(on_ref, oh_ref, wn_ref, wh_ref, x_ref, g_ref, gain_ref, sh_ref, sc_ref, x1_ref, ht_ref):
    mix = (jnp.dot(on_ref[...], wn_ref[...], preferred_element_type=F32)
           + jnp.dot(oh_ref[...], wh_ref[...], preferred_element_type=F32))
    x1 = x_ref[...] + g_ref[0] * mix
    x1_ref[...] = x1
    ht_ref[...] = _rms_mod(x1, gain_ref[...], sc_ref[0], sh_ref[0]).T.astype(BF16)


def out_proj(o_nsa, o_hgrn, w_out, x, gain, mod, layer):
    tm = 256
    half = D_MODEL // 2
    rows = pl.BlockSpec((tm, half), lambda i: (i, 0))
    full = pl.BlockSpec((tm, D_MODEL), lambda i: (i, 0))
    return pl.pallas_call(
        _out_proj_kernel,
        grid=(TOKENS // tm,),
        in_specs=[rows, rows,
                  pl.BlockSpec((half, D_MODEL), lambda i: (0, 0)),
                  pl.BlockSpec((half, D_MODEL), lambda i: (1, 0)),
                  full, _mod_spec(layer, 2, SEQ // tm),
                  pl.BlockSpec((1, D_MODEL), lambda i: (0, 0)),
                  _mod_spec(layer, 3, SEQ // tm), _mod_spec(layer, 4, SEQ // tm)],
        out_specs=[full, pl.BlockSpec((D_MODEL, tm), lambda i: (0, i))],
        out_shape=[jax.ShapeDtypeStruct((TOKENS, D_MODEL), F32), jax.ShapeDtypeStruct((D_MODEL, TOKENS), BF16)],
        compiler_params=_params(("parallel",)),
        name="out_proj",
    )(o_nsa, o_hgrn, w_out, w_out, x, mod, gain.reshape(1, D_MODEL), mod, mod)


def _top_values(x, k):
    vals = []
    for _ in range(k):
        m = jnp.max(x, axis=0, keepdims=True)
        vals.append(m)
        x = jnp.where(x == m, -jnp.inf, x)
    return vals


def _peer_score_kernel(ht_ref, wqt_ref, keys_ref, st_ref, et_ref, tau_ref):
    qt = jnp.dot(wqt_ref[...], ht_ref[...], preferred_element_type=F32)
    taus = []
    for h in range(PEER_HEADS):
        s_half, top = [], []
        for p in range(2):
            hp = 2 * h + p
            s = jnp.dot(keys_ref[hp], qt[hp * PEER_KEYS:(hp + 1) * PEER_KEYS], preferred_element_type=F32)
            st_ref[hp * PEER_KEYS:(hp + 1) * PEER_KEYS, :] = s
            s_half.append(s)
            top.append(_top_values(s, PEER_TOPK))
        b_top = jnp.concatenate(top[1], axis=0)
        cand = jnp.concatenate([a_i + b_top for a_i in top[0]], axis=0)
        best = _top_values(cand, PEER_TOPK)
        z = best[0] - best[0]
        for v in best:
            z = z + jnp.exp(v - best[0])
        taus.append(best[PEER_TOPK - 1])
        et_ref[(2 * h) * PEER_KEYS:(2 * h + 1) * PEER_KEYS, :] = jnp.exp(s_half[0] - top[0][0]) / z
        et_ref[(2 * h + 1) * PEER_KEYS:(2 * h + 2) * PEER_KEYS, :] = jnp.exp(s_half[1] - top[1][0])
    tau_ref[...] = jnp.concatenate(taus, axis=0)


def peer_scores(ht, wqt, keys):
    tm = 256
    rows = PEER_HEADS * 2 * PEER_KEYS
    return pl.pallas_call(
        _peer_score_kernel,
        grid=(TOKENS // tm,),
        in_specs=[pl.BlockSpec((D_MODEL, tm), lambda i: (0, i)),
                  pl.BlockSpec((rows, D_MODEL), lambda i: (0, 0)),
                  pl.BlockSpec((PEER_HEADS * 2, PEER_KEYS, PEER_KEYS), lambda i: (0, 0, 0))],
        out_specs=[pl.BlockSpec((rows, tm), lambda i: (0, i)),
                   pl.BlockSpec((rows, tm), lambda i: (0, i)),
                   pl.BlockSpec((PEER_HEADS, tm), lambda i: (0, i))],
        out_shape=[jax.ShapeDtypeStruct((rows, TOKENS), F32), jax.ShapeDtypeStruct((rows, TOKENS), F32),
                   jax.ShapeDtypeStruct((PEER_HEADS, TOKENS), F32)],
        compiler_params=_params(("parallel",)),
        name="peer_scores",
    )(ht, wqt, keys)


PEER_TM = 512
PEER_TE = 512


def _peer_dense_kernel(xt_ref, st_ref, et_ref, tau_ref, u_ref, vt_ref, o_ref):
    j = pl.program_id(1)

    @pl.when(j == 0)
    def _():
        o_ref[...] = jnp.zeros_like(o_ref)

    act = jax.nn.gelu(jnp.dot(u_ref[...], xt_ref[...], preferred_element_type=F32))
    blocks = []
    for ab in range(PEER_TE // PEER_KEYS):
        a = j * (PEER_TE // PEER_KEYS) + ab
        w = jnp.zeros((PEER_KEYS, PEER_TM), F32)
        for h in range(PEER_HEADS):
            ra = 2 * h * PEER_KEYS + a
            rb = (2 * h + 1) * PEER_KEYS
            sa = st_ref[pl.ds(ra, 1), :]
            ea = et_ref[pl.ds(ra, 1), :]
            sb = st_ref[rb:rb + PEER_KEYS, :]
            eb = et_ref[rb:rb + PEER_KEYS, :]
            w = w + jnp.where(sa + sb >= tau_ref[h:h + 1, :], ea * eb, 0.0)
        blocks.append((act[ab * PEER_KEYS:(ab + 1) * PEER_KEYS] * w).astype(BF16))
    gw = jnp.concatenate(blocks, axis=0)
    o_ref[...] += jnp.dot(vt_ref[...], gw, preferred_element_type=F32)


def peer_dense(ht, st, et, tau, u, vt):
    rows = PEER_HEADS * 2 * PEER_KEYS
    return pl.pallas_call(
        _peer_dense_kernel,
        grid=(TOKENS // PEER_TM, PEER_EXPERTS // PEER_TE),
        in_specs=[pl.BlockSpec((D_MODEL, PEER_TM), lambda i, j: (0, i)),
                  pl.BlockSpec((rows, PEER_TM), lambda i, j: (0, i)),
                  pl.BlockSpec((rows, PEER_TM), lambda i, j: (0, i)),
                  pl.BlockSpec((PEER_HEADS, PEER_TM), lambda i, j: (0, i)),
                  pl.BlockSpec((PEER_TE, D_MODEL), lambda i, j: (j, 0)),
                  pl.BlockSpec((D_MODEL, PEER_TE), lambda i, j: (0, j))],
        out_specs=pl.BlockSpec((D_MODEL, PEER_TM), lambda i, j: (0, i)),
        out_shape=jax.ShapeDtypeStruct((D_MODEL, TOKENS), F32),
        compiler_params=_params(("parallel", "arbitrary")),
        name="peer_dense",
    )(ht, st, et, tau, u, vt)


def _peer_resid_kernel(x_ref, ot_ref, g_ref, gain_ref, sh_ref, sc_ref, x2_ref, h_ref):
    x2 = x_ref[...] + g_ref[0] * ot_ref[...].T
    x2_ref[...] = x2
    h_ref[...] = _rms_mod(x2, gain_ref[...], sc_ref[0], sh_ref[0]).astype(BF16)


def _final_kernel(x_ref, ot_ref, g_ref, gain_ref, o_ref):
    x2 = x_ref[...] + g_ref[0] * ot_ref[...].T
    o_ref[...] = x2 * lax.rsqrt(jnp.mean(x2 * x2, axis=-1, keepdims=True) + EPS) * gain_ref[...]


def peer_residual(x1, out_t, mod, layer, gain_next):
    tm = 256
    full = pl.BlockSpec((tm, D_MODEL), lambda i: (i, 0))
    return pl.pallas_call(
        _peer_resid_kernel,
        grid=(TOKENS // tm,),
        in_specs=[full, pl.BlockSpec((D_MODEL, tm), lambda i: (0, i)), _mod_spec(layer, 5, SEQ // tm),
                  pl.BlockSpec((1, D_MODEL), lambda i: (0, 0)),
                  _mod_spec(layer + 1, 0, SEQ // tm), _mod_spec(layer + 1, 1, SEQ // tm)],
        out_specs=[full, full],
        out_shape=[jax.ShapeDtypeStruct((TOKENS, D_MODEL), F32), jax.ShapeDtypeStruct((TOKENS, D_MODEL), BF16)],
        compiler_params=_params(("parallel",)),
        name="peer_residual",
    )(x1, out_t, mod, gain_next.reshape(1, D_MODEL), mod, mod)


def final_residual_norm(x1, out_t, mod, layer, gain):
    tm = 256
    full = pl.BlockSpec((tm, D_MODEL), lambda i: (i, 0))
    return pl.pallas_call(
        _final_kernel,
        grid=(TOKENS // tm,),
        in_specs=[full, pl.BlockSpec((D_MODEL, tm), lambda i: (0, i)), _mod_spec(layer, 5, SEQ // tm),
                  pl.BlockSpec((1, D_MODEL), lambda i: (0, 0))],
        out_specs=full,
        out_shape=jax.ShapeDtypeStruct((TOKENS, D_MODEL), F32),
        compiler_params=_params(("parallel",)),
        name="final_norm",
    )(x1, out_t, mod, gain.reshape(1, D_MODEL))


def _rope_tables(positions):
    inv = ROPE_THETA ** (-jnp.arange(0, 2 * ROT_HALF, 2, dtype=F32) / (2 * ROT_HALF))
    ang = positions.astype(F32).reshape(TOKENS, 1) * inv
    cos, sin = jnp.cos(ang), jnp.sin(ang)
    rest = HEAD_DIM - 2 * ROT_HALF
    cs = jnp.concatenate([cos, cos, jnp.ones((TOKENS, rest), F32)], axis=1)
    sn = jnp.concatenate([-sin, sin, jnp.zeros((TOKENS, rest), F32)], axis=1)
    return cs, sn


def kernel(x, c, positions, norm_mix, norm_ffn, w_ada, b_ada, w_in, w_out, cmp_k_pos, cmp_k_w1, cmp_k_b1, cmp_k_w2, cmp_k_b2, cmp_v_pos, cmp_v_w1, cmp_v_b1, cmp_v_w2, cmp_v_b2, hgrn_norm, hgrn_lb, peer_wq, peer_keys, peer_u, peer_v, final_norm):
    xf = x.reshape(TOKENS, D_MODEL)
    rope_cs, rope_sn = _rope_tables(positions)
    c8 = jnp.concatenate([c, jnp.zeros((8 - BATCH, D_MODEL), F32)], axis=0)
    mod = ada_mod(c8, w_ada, b_ada).reshape(DEPTH * 8 * 6, 1, D_MODEL)
    lb_all = jnp.cumsum(jax.nn.softmax(hgrn_lb.astype(F32), axis=0), axis=0)
    lb_all = lb_all - lb_all[0:1]

    h = adaln(xf, norm_mix[0], mod, 0)
    for l in range(DEPTH):
        proj = in_proj(h, pack_w_in(w_in[l]))
        cmp_kv = compress(
            proj,
            jnp.stack([cmp_k_pos[l], cmp_v_pos[l]]),
            jnp.stack([cmp_k_w1[l], cmp_v_w1[l]]).astype(BF16),
            jnp.stack([cmp_k_b1[l], cmp_v_b1[l]]).reshape(2, 1, CMP_HIDDEN),
            jnp.stack([cmp_k_w2[l], cmp_v_w2[l]]).astype(BF16),
            jnp.stack([cmp_k_b2[l], cmp_v_b2[l]]).reshape(2, 1, HEAD_DIM))
        o_nsa = nsa_attention(proj, cmp_kv, rope_cs, rope_sn)
        o_hgrn = hgrn(proj, lb_all[l], hgrn_norm[l])
        x1, ht = out_proj(o_nsa, o_hgrn, w_out[l].astype(BF16), xf, norm_ffn[l], mod, l)
        st, et, tau = peer_scores(ht, peer_wq[l].T.astype(BF16),
                                  peer_keys[l].reshape(PEER_HEADS * 2, PEER_KEYS, PEER_KEYS))
        out_t = peer_dense(ht, st, et, tau, peer_u[l].astype(BF16), peer_v[l].T.astype(BF16))
        if l + 1 < DEPTH:
            xf, h = peer_residual(x1, out_t, mod, l, norm_mix[l + 1])
        else:
            xf = final_residual_norm(x1, out_t, mod, l, final_norm)
    return xf.reshape(BATCH, SEQ, D_MODEL)
```

```python
import functools

import numpy as np
import jax
import jax.numpy as jnp
from jax import lax
from jax.experimental import pallas as pl
from jax.experimental.pallas import tpu as pltpu

F32 = jnp.float32
BF16 = jnp.bfloat16

D_MODEL = 2048
BATCH = 4
SEQ = 2048
TOKENS = BATCH * SEQ
DEPTH = 2
EPS = 1e-6

HEAD_DIM = 128
NSA_HEADS = 8
NSA_GROUPS = 2
NSA_HPG = NSA_HEADS // NSA_GROUPS
CMP_BLOCK = 32
CMP_STRIDE = 16
CMP_HIDDEN = 256
N_CMP = (SEQ - CMP_BLOCK) // CMP_STRIDE + 1
N_CMP_PAD = 128
SLC_BLOCK = 64
N_SLC = SEQ // SLC_BLOCK
SLC_TOPN = 16
N_LOCAL_FORCED = 2
WINDOW = 512
ROT_HALF = 16
ROPE_THETA = 500000.0
Q_TILE = 128
WIN_KEYS = WINDOW + Q_TILE

HGRN_HEADS = 8
HGRN_CHUNK = 64
HGRN_SUB = 16

PEER_HEADS = 8
PEER_KEYS = 128
PEER_EXPERTS = PEER_KEYS * PEER_KEYS
PEER_TOPK = 16

IN_WIDTH_PACKED = 6912
COL_Q, COL_KC, COL_VC, COL_KS, COL_VS, COL_KW, COL_VW = 0, 8, 10, 12, 14, 16, 18
COL_HQ, COL_HF, COL_HI, COL_HG, COL_GL = 20, 28, 36, 44, 52

VMEM_LIMIT = 56 * 1024 * 1024


def _params(sem):
    return pltpu.CompilerParams(dimension_semantics=sem, vmem_limit_bytes=VMEM_LIMIT)


def _ada_kernel(c_ref, w_ref, b_ref, o_ref):
    c = c_ref[...]
    o_ref[0] = jnp.dot(c * jax.nn.sigmoid(c), w_ref[0], preferred_element_type=F32) + b_ref[0]


def ada_mod(c8, w_ada, b_ada):
    tn = 1024
    n = 6 * D_MODEL
    return pl.pallas_call(
        _ada_kernel,
        grid=(DEPTH, n // tn),
        in_specs=[
            pl.BlockSpec((8, D_MODEL), lambda l, j: (0, 0)),
            pl.BlockSpec((1, D_MODEL, tn), lambda l, j: (l, 0, j)),
            pl.BlockSpec((1, 1, tn), lambda l, j: (l, 0, j)),
        ],
        out_specs=pl.BlockSpec((1, 8, tn), lambda l, j: (l, 0, j)),
        out_shape=jax.ShapeDtypeStruct((DEPTH, 8, n), F32),
        compiler_params=_params(("parallel", "parallel")),
        name="ada_mod",
    )(c8, w_ada, b_ada.reshape(DEPTH, 1, n))


def _mod_spec(layer, k, rows_per_batch):
    return pl.BlockSpec((1, 1, D_MODEL), lambda i: ((layer * 8 + i // rows_per_batch) * 6 + k, 0, 0))


def _rms_mod(x, gain, scale, shift):
    y = x * lax.rsqrt(jnp.mean(x * x, axis=-1, keepdims=True) + EPS) * gain
    return y * (1.0 + scale) + shift


def _adaln_kernel(x_ref, gain_ref, sh_ref, sc_ref, h_ref):
    h_ref[...] = _rms_mod(x_ref[...], gain_ref[...], sc_ref[0], sh_ref[0]).astype(BF16)


def adaln(x, gain, mod, layer):
    tm = 256
    return pl.pallas_call(
        _adaln_kernel,
        grid=(TOKENS // tm,),
        in_specs=[
            pl.BlockSpec((tm, D_MODEL), lambda i: (i, 0)),
            pl.BlockSpec((1, D_MODEL), lambda i: (0, 0)),
            _mod_spec(layer, 0, SEQ // tm),
            _mod_spec(layer, 1, SEQ // tm),
        ],
        out_specs=pl.BlockSpec((tm, D_MODEL), lambda i: (i, 0)),
        out_shape=jax.ShapeDtypeStruct((TOKENS, D_MODEL), BF16),
        compiler_params=_params(("parallel",)),
        name="adaln",
    )(x, gain.reshape(1, D_MODEL), mod, mod)


def _matmul_kernel(a_ref, b_ref, o_ref):
    o_ref[...] = jnp.dot(a_ref[...], b_ref[...], preferred_element_type=F32)


def in_proj(h, w, layer):
    tm, tn = 1024, 768
    return pl.pallas_call(
        _matmul_kernel,
        grid=(TOKENS // tm, IN_WIDTH_PACKED // tn),
        in_specs=[
            pl.BlockSpec((tm, D_MODEL), lambda i, j: (i, 0)),
            pl.BlockSpec((None, D_MODEL, tn), lambda i, j: (layer, 0, j)),
        ],
        out_specs=pl.BlockSpec((tm, tn), lambda i, j: (i, j)),
        out_shape=jax.ShapeDtypeStruct((TOKENS, IN_WIDTH_PACKED), F32),
        compiler_params=_params(("parallel", "parallel")),
        name="in_proj",
    )(h, w)


def pack_w_in(w):
    gl = w[..., 2560:2584]
    z = jnp.zeros(w.shape[:-1] + (116,), w.dtype)
    return jnp.concatenate([w[..., :2560], w[..., 2584:], gl[..., :12], z, gl[..., 12:], z], axis=-1).astype(BF16)


def _compress_kernel(t_ref, pos_ref, w1_ref, b1_ref, w2_ref, b2_ref, o_ref, pad_ref):
    pad_ref[0:SEQ, :] = t_ref[...]
    pad_ref[SEQ:SEQ + CMP_BLOCK, :] = jnp.zeros((CMP_BLOCK, HEAD_DIM), F32)
    acc = jnp.zeros((N_CMP_PAD, CMP_HIDDEN), F32)
    for l in range(CMP_BLOCK):
        tl = pad_ref[pl.ds(l, N_CMP_PAD, stride=CMP_STRIDE), :] + pos_ref[0, l:l + 1, :]
        acc = acc + jnp.dot(tl.astype(BF16), w1_ref[0, l * HEAD_DIM:(l + 1) * HEAD_DIM, :],
                            preferred_element_type=F32)
    hid = jax.nn.gelu(acc + b1_ref[0])
    o_ref[0, 0] = jnp.dot(hid.astype(BF16), w2_ref[0], preferred_element_type=F32) + b2_ref[0]


def compress(proj, pos, w1, b1, w2, b2, layer):
    return pl.pallas_call(
        _compress_kernel,
        grid=(2, BATCH, NSA_GROUPS),
        in_specs=[
            pl.BlockSpec((SEQ, HEAD_DIM), lambda kv, b, g: (b, COL_KC + 2 * kv + g)),
            pl.BlockSpec((None, 1, CMP_BLOCK, HEAD_DIM), lambda kv, b, g: (layer, kv, 0, 0)),
            pl.BlockSpec((None, 1, CMP_BLOCK * HEAD_DIM, CMP_HIDDEN), lambda kv, b, g: (layer, kv, 0, 0)),
            pl.BlockSpec((None, 1, 1, CMP_HIDDEN), lambda kv, b, g: (layer, kv, 0, 0)),
            pl.BlockSpec((None, 1, CMP_HIDDEN, HEAD_DIM), lambda kv, b, g: (layer, kv, 0, 0)),
            pl.BlockSpec((None, 1, 1, HEAD_DIM), lambda kv, b, g: (layer, kv, 0, 0)),
        ],
        out_specs=pl.BlockSpec((1, 1, N_CMP_PAD, HEAD_DIM), lambda kv, b, g: (kv, b * NSA_GROUPS + g, 0, 0)),
        out_shape=jax.ShapeDtypeStruct((2, BATCH * NSA_GROUPS, N_CMP_PAD, HEAD_DIM), F32),
        scratch_shapes=[pltpu.VMEM((SEQ + CMP_BLOCK, HEAD_DIM), F32)],
        compiler_params=_params(("arbitrary", "arbitrary", "arbitrary")),
        name="nsa_compress",
    )(proj, pos, w1, b1, w2, b2)


def _rope(t, cs, sn):
    lane = lax.broadcasted_iota(jnp.int32, t.shape, 1)
    swapped = jnp.where(lane < ROT_HALF, pltpu.roll(t, HEAD_DIM - ROT_HALF, axis=1), pltpu.roll(t, ROT_HALF, axis=1))
    return t * cs + swapped * sn


def _dot_nt(a, b):
    return lax.dot_general(a, b, (((1,), (1,)), ((), ())), preferred_element_type=F32)


MASKED = -1e30
GROUP_TILES = 2


def _nsa_kernel(q_ref, kc_ref, vc_ref, ks_ref, vs_ref, kw_ref, vw_ref, gl_ref, cs_ref, sn_ref, wov_ref,
                o_ref, kaug, vaug, kwr, vwaug):
    scale = HEAD_DIM ** -0.5
    neg_inf = -jnp.inf
    row = lax.broadcasted_iota(jnp.int32, (SEQ, HEAD_DIM), 0)
    lane = lax.broadcasted_iota(jnp.int32, (SEQ, HEAD_DIM), 1)
    kaug[:, 0:HEAD_DIM] = _rope(ks_ref[...], cs_ref[...], sn_ref[...]).astype(BF16)
    kaug[:, HEAD_DIM:2 * HEAD_DIM] = jnp.where(lane == (row >> 6), 1.0, 0.0).astype(BF16)
    kwr[...] = _rope(kw_ref[...], cs_ref[...], sn_ref[...]).astype(BF16)
    ones = jnp.ones((SEQ, HEAD_DIM), BF16)
    vaug[:, 0:HEAD_DIM] = vs_ref[...].astype(BF16)
    vaug[:, HEAD_DIM:2 * HEAD_DIM] = ones
    vwaug[:, 0:HEAD_DIM] = vw_ref[...].astype(BF16)
    vwaug[:, HEAD_DIM:2 * HEAD_DIM] = ones
    kcb = kc_ref[0, 0].astype(BF16)
    vcb = vc_ref[0, 0].astype(BF16)

    def tile(i, extent):
        t0 = pl.multiple_of(i * Q_TILE, Q_TILE)
        q4 = q_ref[pl.ds(t0, Q_TILE), :] * scale
        cs = cs_ref[pl.ds(t0, Q_TILE), :]
        sn = sn_ref[pl.ds(t0, Q_TILE), :]
        gate = jax.nn.sigmoid(gl_ref[pl.ds(t0, Q_TILE), :])
        qh = [q4[:, h * HEAD_DIM:(h + 1) * HEAD_DIM] for h in range(NSA_HPG)]

        qs = jnp.concatenate(qh, axis=0).astype(BF16)
        sc = _dot_nt(qs, kcb)
        rows = lax.broadcasted_iota(jnp.int32, sc.shape, 0)
        ncol = lax.broadcasted_iota(jnp.int32, sc.shape, 1)
        tok = t0 + (rows & (Q_TILE - 1))
        valid = ncol * CMP_STRIDE + (CMP_BLOCK - 1) <= tok
        s = jnp.where(valid, sc, neg_inf)
        m = jnp.max(s, axis=-1, keepdims=True)
        m = jnp.where(m == neg_inf, 0.0, m)
        e = jnp.where(valid, jnp.exp(s - m), 0.0)
        p = e / jnp.maximum(jnp.sum(e, axis=-1, keepdims=True), 1e-30)
        o_cmp = jnp.dot(p.astype(BF16), vcb, preferred_element_type=F32)

        p4 = p[0:Q_TILE] + p[Q_TILE:2 * Q_TILE] + p[2 * Q_TILE:3 * Q_TILE] + p[3 * Q_TILE:4 * Q_TILE]
        imp = jnp.dot(p4, wov_ref[...], preferred_element_type=F32)
        jj = lax.broadcasted_iota(jnp.int32, imp.shape, 1)
        tt = t0 + lax.broadcasted_iota(jnp.int32, imp.shape, 0)
        back = (tt >> 6) - jj
        forced = (jj == 0) | (back.astype(jnp.uint32) < N_LOCAL_FORCED)
        score = jnp.where(forced, jnp.inf, jnp.where(jj * SLC_BLOCK <= tt, imp, neg_inf))
        score_t = score.T[0:N_SLC]
        jrow = lax.broadcasted_iota(jnp.int32, score_t.shape, 0)
        rank = jnp.zeros(score_t.shape, F32)
        for jp in range(N_SLC):
            other = score_t[jp:jp + 1]
            later = jnp.where(jrow > jp, 1.0, 0.0)
            rank = rank + jnp.where(other > score_t, 1.0, jnp.where(other == score_t, later, 0.0))
        bias_t = jnp.where(rank < SLC_TOPN, 0.0, MASKED)
        sel_bias = jnp.concatenate([bias_t, jnp.zeros((HEAD_DIM - N_SLC, Q_TILE), F32)], axis=0).T

        tail = GROUP_TILES * Q_TILE
        rows4 = NSA_HPG * Q_TILE
        kpos = (extent - tail) + lax.broadcasted_iota(jnp.int32, (rows4, tail), 1)
        ok_tail = kpos <= t0 + (lax.broadcasted_iota(jnp.int32, (rows4, tail), 0) & (Q_TILE - 1))

        qr = jnp.concatenate([_rope(qh[h], cs, sn) for h in range(NSA_HPG)], axis=0)
        qa = jnp.concatenate([qr, jnp.concatenate([sel_bias] * NSA_HPG, axis=0)], axis=1).astype(BF16)
        s1 = _dot_nt(qa, kaug[0:extent, :])
        s1_tail = jnp.where(ok_tail, s1[:, extent - tail:], MASKED)
        s1 = s1_tail if extent == tail else jnp.concatenate([s1[:, :extent - tail], s1_tail], axis=1)
        e1 = jnp.exp(s1 - jnp.max(s1, axis=-1, keepdims=True)).astype(BF16)
        r1 = jnp.dot(e1, vaug[0:extent, :], preferred_element_type=F32)
        o_slc = r1[:, :HEAD_DIM] / r1[:, HEAD_DIM:HEAD_DIM + 1]

        ws = pl.multiple_of(jnp.clip(t0 - WINDOW, 0, SEQ - WIN_KEYS), Q_TILE)
        wk = ws + lax.broadcasted_iota(jnp.int32, (rows4, WIN_KEYS), 1)
        wt = t0 + (lax.broadcasted_iota(jnp.int32, (rows4, WIN_KEYS), 0) & (Q_TILE - 1))
        ok_win = (wt - wk).astype(jnp.uint32) < WINDOW
        s2 = jnp.where(ok_win, _dot_nt(qr.astype(BF16), kwr[pl.ds(ws, WIN_KEYS), :]), MASKED)
        e2 = jnp.exp(s2 - jnp.max(s2, axis=-1, keepdims=True)).astype(BF16)
        r2 = jnp.dot(e2, vwaug[pl.ds(ws, WIN_KEYS), :], preferred_element_type=F32)
        o_win = r2[:, :HEAD_DIM] / r2[:, HEAD_DIM:HEAD_DIM + 1]

        outs = []
        for h in range(NSA_HPG):
            hr = slice(h * Q_TILE, (h + 1) * Q_TILE)
            outs.append(gate[:, 3 * h:3 * h + 1] * o_cmp[hr] + gate[:, 3 * h + 1:3 * h + 2] * o_slc[hr]
                        + gate[:, 3 * h + 2:3 * h + 3] * o_win[hr])
        o_ref[pl.ds(t0, Q_TILE), :] = jnp.concatenate(outs, axis=1).astype(BF16)

    for grp in range(SEQ // (GROUP_TILES * Q_TILE)):
        extent = (grp + 1) * GROUP_TILES * Q_TILE

        def body(ii, carry, grp=grp, extent=extent):
            tile(grp * GROUP_TILES + ii, extent)
            return carry

        lax.fori_loop(0, GROUP_TILES, body, 0)


def _cmp_to_slc_weights():
    cs = np.arange(N_CMP) * CMP_STRIDE
    ce = cs + CMP_BLOCK
    ss = np.arange(N_SLC) * SLC_BLOCK
    se = ss + SLC_BLOCK
    ov = np.clip(np.minimum(ce[:, None], se[None, :]) - np.maximum(cs[:, None], ss[None, :]), 0, None)
    w = np.zeros((N_CMP_PAD, HEAD_DIM), np.float32)
    w[:N_CMP, :N_SLC] = ov / CMP_BLOCK
    return w


def nsa_attention(proj, cmp_kv, rope_cs, rope_sn):
    gw = NSA_HPG * HEAD_DIM
    col = lambda c: pl.BlockSpec((SEQ, HEAD_DIM), lambda b, g: (b, c + g))
    return pl.pallas_call(
        _nsa_kernel,
        grid=(BATCH, NSA_GROUPS),
        in_specs=[
            pl.BlockSpec((SEQ, gw), lambda b, g: (b, g)),
            pl.BlockSpec((1, 1, N_CMP_PAD, HEAD_DIM), lambda b, g: (0, b * NSA_GROUPS + g, 0, 0)),
            pl.BlockSpec((1, 1, N_CMP_PAD, HEAD_DIM), lambda b, g: (1, b * NSA_GROUPS + g, 0, 0)),
            col(COL_KS), col(COL_VS), col(COL_KW), col(COL_VW), col(COL_GL),
            pl.BlockSpec((SEQ, HEAD_DIM), lambda b, g: (b, 0)),
            pl.BlockSpec((SEQ, HEAD_DIM), lambda b, g: (b, 0)),
            pl.BlockSpec((N_CMP_PAD, HEAD_DIM), lambda b, g: (0, 0)),
        ],
        out_specs=pl.BlockSpec((SEQ, gw), lambda b, g: (b, g)),
        out_shape=jax.ShapeDtypeStruct((TOKENS, NSA_HEADS * HEAD_DIM), BF16),
        scratch_shapes=[pltpu.VMEM((SEQ, 2 * HEAD_DIM), BF16), pltpu.VMEM((SEQ, 2 * HEAD_DIM), BF16),
                        pltpu.VMEM((SEQ, HEAD_DIM), BF16), pltpu.VMEM((SEQ, 2 * HEAD_DIM), BF16)],
        compiler_params=_params(("parallel", "parallel")),
        name="nsa_attention",
    )(proj, cmp_kv, cmp_kv, proj, proj, proj, proj, proj, rope_cs, rope_sn, jnp.asarray(_cmp_to_slc_weights()))


HGRN_HEADS_PER_STEP = 4


def _hgrn_chunk(qc, fc, vc, gc, lb, log_lb, log_1m, gn, tri, state_t):
    C, SUB = HGRN_CHUNK, HGRN_SUB
    row_sub = lax.broadcasted_iota(jnp.int32, (SUB, HEAD_DIM), 0)
    qh = qc * jax.nn.sigmoid(qc)
    log_sig = jnp.minimum(fc, 0.0) - jnp.log(1.0 + jnp.exp(-jnp.abs(fc)))
    b = log_1m + log_sig
    lf = jnp.maximum(log_lb, b) + jnp.log(1.0 + jnp.exp(-jnp.abs(log_lb - b)))
    kh = (1.0 - lb) * jax.nn.sigmoid(-fc)
    a = jnp.dot(tri, lf, preferred_element_type=F32)
    inter = _dot_nt(qh * jnp.exp(a), state_t)
    parts = []
    for blk in range(C // SUB):
        lo = blk * SUB
        ab, qb, kb, vb = a[lo:lo + SUB], qh[lo:lo + SUB], kh[lo:lo + SUB], vc[lo:lo + SUB]
        acc = inter[lo:lo + SUB]
        if blk > 0:
            a_ref_row = a[lo - 1:lo]
            qt = qb * jnp.exp(ab - a_ref_row)
            kt = kh[0:lo] * jnp.exp(a_ref_row - a[0:lo])
            acc = acc + jnp.dot(_dot_nt(qt, kt), vc[0:lo], preferred_element_type=F32)
        for s in range(SUB):
            dec = jnp.exp(jnp.where(row_sub >= s, ab - ab[s:s + 1], -jnp.inf))
            w = jnp.sum(qb * kb[s:s + 1] * dec, axis=-1, keepdims=True)
            acc = acc + w * vb[s:s + 1]
        parts.append(acc)
    o = jnp.concatenate(parts, axis=0)
    a_end = a[C - 1:C]
    kt_end = kh * jnp.exp(a_end - a)
    state_t = state_t * jnp.exp(a_end) + lax.dot_general(vc, kt_end, (((0,), (0,)), ((), ())),
                                                          preferred_element_type=F32)
    o = o * lax.rsqrt(jnp.mean(o * o, axis=-1, keepdims=True) + EPS) * gn
    return (o * jax.nn.sigmoid(gc)).astype(BF16), state_t


def _hgrn_kernel(q_ref, f_ref, i_ref, g_ref, lb_ref, gn_ref, tri_ref, o_ref):
    C, HP = HGRN_CHUNK, HGRN_HEADS_PER_STEP
    lbs = [lb_ref[hp] for hp in range(HP)]
    log_lbs = [jnp.log(lb) for lb in lbs]
    log_1ms = [jnp.log(1.0 - lb) for lb in lbs]
    gns = [gn_ref[hp] for hp in range(HP)]

    def chunk(c, states):
        r0 = pl.multiple_of(c * C, C)
        qc = q_ref[pl.ds(r0, C), :]
        fc = f_ref[pl.ds(r0, C), :]
        vc = i_ref[pl.ds(r0, C), :]
        gc = g_ref[pl.ds(r0, C), :]
        outs, new_states = [], []
        for hp in range(HP):
            cols = slice(hp * HEAD_DIM, (hp + 1) * HEAD_DIM)
            o, st = _hgrn_chunk(qc[:, cols], fc[:, cols], vc[:, cols], gc[:, cols], lbs[hp], log_lbs[hp], log_1ms[hp],
                                gns[hp], tri_ref[...], states[hp])
            outs.append(o)
            new_states.append(st)
        o_ref[pl.ds(r0, C), :] = jnp.concatenate(outs, axis=1)
        return tuple(new_states)

    lax.fori_loop(0, SEQ // C, chunk, tuple(jnp.zeros((HEAD_DIM, HEAD_DIM), F32) for _ in range(HP)))


def hgrn(proj, lb, gain):
    hp = HGRN_HEADS_PER_STEP
    col = lambda c: pl.BlockSpec((SEQ, hp * HEAD_DIM), lambda b, h: (b, c // hp + h))
    vec = pl.BlockSpec((hp, 1, HEAD_DIM), lambda b, h: (h, 0, 0))
    tri = np.tril(np.ones((HGRN_CHUNK, HGRN_CHUNK), np.float32))
    return pl.pallas_call(
        _hgrn_kernel,
        grid=(BATCH, HGRN_HEADS // hp),
        in_specs=[col(COL_HQ), col(COL_HF), col(COL_HI), col(COL_HG), vec, vec,
                  pl.BlockSpec((HGRN_CHUNK, HGRN_CHUNK), lambda b, h: (0, 0))],
        out_specs=pl.BlockSpec((SEQ, hp * HEAD_DIM), lambda b, h: (b, h)),
        out_shape=jax.ShapeDtypeStruct((TOKENS, HGRN_HEADS * HEAD_DIM), BF16),
        compiler_params=_params(("parallel", "parallel")),
        name="hgrn2",
    )(proj, proj, proj, proj, lb.reshape(HGRN_HEADS, 1, HEAD_DIM), gain.reshape(HGRN_HEADS, 1, HEAD_DIM),
      jnp.asarray(tri))


def _out_proj_kernel(on_ref, oh_ref, wn_ref, wh_ref, x_ref, g_ref, gain_ref, sh_ref, sc_ref, x1_ref, ht_ref):
    mix = (jnp.dot(on_ref[...], wn_ref[...], preferred_element_type=F32)
           + jnp.dot(oh_ref[...], wh_ref[...], preferred_element_type=F32))
    x1 = x_ref[...] + g_ref[0] * mix
    x1_ref[...] = x1
    ht_ref[...] = _rms_mod(x1, gain_ref[...], sc_ref[0], sh_ref[0]).T.astype(BF16)


def out_proj(o_nsa, o_hgrn, w_out, x, gain, mod, layer):
    tm = 256
    half = D_MODEL // 2
    rows = pl.BlockSpec((tm, half), lambda i: (i, 0))
    full = pl.BlockSpec((tm, D_MODEL), lambda i: (i, 0))
    return pl.pallas_call(
        _out_proj_kernel,
        grid=(TOKENS // tm,),
        in_specs=[rows, rows,
                  pl.BlockSpec((None, half, D_MODEL), lambda i: (layer, 0, 0)),
                  pl.BlockSpec((None, half, D_MODEL), lambda i: (layer, 1, 0)),
                  full, _mod_spec(layer, 2, SEQ // tm),
                  pl.BlockSpec((1, D_MODEL), lambda i: (0, 0)),
                  _mod_spec(layer, 3, SEQ // tm), _mod_spec(layer, 4, SEQ // tm)],
        out_specs=[full, pl.BlockSpec((D_MODEL, tm), lambda i: (0, i))],
        out_shape=[jax.ShapeDtypeStruct((TOKENS, D_MODEL), F32), jax.ShapeDtypeStruct((D_MODEL, TOKENS), BF16)],
        compiler_params=_params(("parallel",)),
        name="out_proj",
    )(o_nsa, o_hgrn, w_out, w_out, x, mod, gain.reshape(1, D_MODEL), mod, mod)


def _top_values(x, k):
    vals = []
    for _ in range(k):
        m = jnp.max(x, axis=0, keepdims=True)
        vals.append(m)
        x = jnp.where(x == m, -jnp.inf, x)
    return vals


def _peer_score_kernel(ht_ref, wqt_ref, keys_ref, eb_ref, ea_ref, thr_ref):
    qt = jnp.dot(wqt_ref[...], ht_ref[...], preferred_element_type=F32)
    for h in range(PEER_HEADS):
        s_half, top = [], []
        for p in range(2):
            hp = 2 * h + p
            s = jnp.dot(keys_ref[hp], qt[hp * PEER_KEYS:(hp + 1) * PEER_KEYS], preferred_element_type=F32)
            s_half.append(s)
            top.append(_top_values(s, PEER_TOPK + 1))
        pad = [jnp.full_like(top[1][0], -jnp.inf)] * 7
        a24 = jnp.concatenate(top[0] + pad, axis=0)
        b24 = jnp.concatenate(top[1] + pad, axis=0)
        cand = jnp.concatenate([a24 + top[1][0]] + [a24[0:8] + top[1][j] for j in range(1, 8)]
                               + [top[0][0] + b24[8:24]], axis=0)
        best = _top_values(cand, PEER_TOPK + 1)
        z = jnp.zeros_like(best[0])
        for v in best[:PEER_TOPK]:
            z = z + jnp.exp(v - best[0])
        tau = 0.5 * (best[PEER_TOPK - 1] + best[PEER_TOPK])
        eb_ref[h * PEER_KEYS:(h + 1) * PEER_KEYS, :] = jnp.exp(s_half[1] - top[1][0])
        ea_ref[h] = jnp.exp(s_half[0] - top[0][0]) / z
        thr_ref[h] = jnp.exp((tau - top[1][0]) - s_half[0])


def peer_scores(ht, wqt, keys, layer):
    tm = 256
    rows = PEER_HEADS * PEER_KEYS
    spec3 = pl.BlockSpec((PEER_HEADS, PEER_KEYS, tm), lambda i: (0, 0, i))
    shape3 = jax.ShapeDtypeStruct((PEER_HEADS, PEER_KEYS, TOKENS), F32)
    return pl.pallas_call(
        _peer_score_kernel,
        grid=(TOKENS // tm,),
        in_specs=[pl.BlockSpec((D_MODEL, tm), lambda i: (0, i)),
                  pl.BlockSpec((None, 2 * rows, D_MODEL), lambda i: (layer, 0, 0)),
                  pl.BlockSpec((None, PEER_HEADS * 2, PEER_KEYS, PEER_KEYS), lambda i: (layer, 0, 0, 0))],
        out_specs=[pl.BlockSpec((rows, tm), lambda i: (0, i)), spec3, spec3],
        out_shape=[jax.ShapeDtypeStruct((rows, TOKENS), F32), shape3, shape3],
        compiler_params=_params(("parallel",)),
        name="peer_scores",
    )(ht, wqt, keys)


PEER_TM = 512
PEER_TE = 1024


def _peer_dense_kernel(xt_ref, eb_ref, ea_ref, thr_ref, u_ref, vt_ref, o_ref):
    @pl.when(pl.program_id(1) == 0)
    def _():
        o_ref[...] = jnp.zeros_like(o_ref)

    act = jax.nn.gelu(jnp.dot(u_ref[...], xt_ref[...], preferred_element_type=F32))
    blocks = []
    for ab in range(PEER_TE // PEER_KEYS):
        w = jnp.zeros((PEER_KEYS, PEER_TM), F32)
        for h in range(PEER_HEADS):
            eb = eb_ref[h * PEER_KEYS:(h + 1) * PEER_KEYS, :]
            w = w + jnp.where(eb >= thr_ref[h, ab:ab + 1, :], eb, 0.0) * ea_ref[h, ab:ab + 1, :]
        blocks.append((act[ab * PEER_KEYS:(ab + 1) * PEER_KEYS] * w).astype(BF16))
    gw = jnp.concatenate(blocks, axis=0)
    o_ref[...] += jnp.dot(vt_ref[...], gw, preferred_element_type=F32)


def peer_dense(ht, eb, ea, thr, u, vt, layer):
    ab = PEER_TE // PEER_KEYS
    rows = pl.BlockSpec((PEER_HEADS, ab, PEER_TM), lambda i, j: (0, j, i))
    return pl.pallas_call(
        _peer_dense_kernel,
        grid=(TOKENS // PEER_TM, PEER_EXPERTS // PEER_TE),
        in_specs=[pl.BlockSpec((D_MODEL, PEER_TM), lambda i, j: (0, i)),
                  pl.BlockSpec((PEER_HEADS * PEER_KEYS, PEER_TM), lambda i, j: (0, i)),
                  rows, rows,
                  pl.BlockSpec((None, PEER_TE, D_MODEL), lambda i, j: (layer, j, 0)),
                  pl.BlockSpec((None, D_MODEL, PEER_TE), lambda i, j: (layer, 0, j))],
        out_specs=pl.BlockSpec((D_MODEL, PEER_TM), lambda i, j: (0, i)),
        out_shape=jax.ShapeDtypeStruct((D_MODEL, TOKENS), F32),
        compiler_params=_params(("parallel", "arbitrary")),
        name="peer_dense",
    )(ht, eb, ea, thr, u, vt)


def _peer_resid_kernel(x_ref, ot_ref, g_ref, gain_ref, sh_ref, sc_ref, x2_ref, h_ref):
    x2 = x_ref[...] + g_ref[0] * ot_ref[...].T
    x2_ref[...] = x2
    h_ref[...] = _rms_mod(x2, gain_ref[...], sc_ref[0], sh_ref[0]).astype(BF16)


def _final_kernel(x_ref, ot_ref, g_ref, gain_ref, o_ref):
    x2 = x_ref[...] + g_ref[0] * ot_ref[...].T
    o_ref[...] = x2 * lax.rsqrt(jnp.mean(x2 * x2, axis=-1, keepdims=True) + EPS) * gain_ref[...]


def peer_residual(x1, out_t, mod, layer, gain_next):
    tm = 256
    full = pl.BlockSpec((tm, D_MODEL), lambda i: (i, 0))
    return pl.pallas_call(
        _peer_resid_kernel,
        grid=(TOKENS // tm,),
        in_specs=[full, pl.BlockSpec((D_MODEL, tm), lambda i: (0, i)), _mod_spec(layer, 5, SEQ // tm),
                  pl.BlockSpec((1, D_MODEL), lambda i: (0, 0)),
                  _mod_spec(layer + 1, 0, SEQ // tm), _mod_spec(layer + 1, 1, SEQ // tm)],
        out_specs=[full, full],
        out_shape=[jax.ShapeDtypeStruct((TOKENS, D_MODEL), F32), jax.ShapeDtypeStruct((TOKENS, D_MODEL), BF16)],
        compiler_params=_params(("parallel",)),
        name="peer_residual",
    )(x1, out_t, mod, gain_next.reshape(1, D_MODEL), mod, mod)


def final_residual_norm(x1, out_t, mod, layer, gain):
    tm = 256
    full = pl.BlockSpec((tm, D_MODEL), lambda i: (i, 0))
    return pl.pallas_call(
        _final_kernel,
        grid=(TOKENS // tm,),
        in_specs=[full, pl.BlockSpec((D_MODEL, tm), lambda i: (0, i)), _mod_spec(layer, 5, SEQ // tm),
                  pl.BlockSpec((1, D_MODEL), lambda i: (0, 0))],
        out_specs=full,
        out_shape=jax.ShapeDtypeStruct((TOKENS, D_MODEL), F32),
        compiler_params=_params(("parallel",)),
        name="final_norm",
    )(x1, out_t, mod, gain.reshape(1, D_MODEL))


def _rope_tables(positions):
    inv = ROPE_THETA ** (-jnp.arange(0, 2 * ROT_HALF, 2, dtype=F32) / (2 * ROT_HALF))
    ang = positions.astype(F32).reshape(TOKENS, 1) * inv
    cos, sin = jnp.cos(ang), jnp.sin(ang)
    rest = HEAD_DIM - 2 * ROT_HALF
    cs = jnp.concatenate([cos, cos, jnp.ones((TOKENS, rest), F32)], axis=1)
    sn = jnp.concatenate([-sin, sin, jnp.zeros((TOKENS, rest), F32)], axis=1)
    return cs, sn


def kernel(x, c, positions, norm_mix, norm_ffn, w_ada, b_ada, w_in, w_out, cmp_k_pos, cmp_k_w1, cmp_k_b1, cmp_k_w2, cmp_k_b2, cmp_v_pos, cmp_v_w1, cmp_v_b1, cmp_v_w2, cmp_v_b2, hgrn_norm, hgrn_lb, peer_wq, peer_keys, peer_u, peer_v, final_norm):
    xf = x.reshape(TOKENS, D_MODEL)
    rope_cs, rope_sn = _rope_tables(positions)
    c8 = jnp.concatenate([c, jnp.zeros((8 - BATCH, D_MODEL), F32)], axis=0)
    mod = ada_mod(c8, w_ada, b_ada).reshape(DEPTH * 8 * 6, 1, D_MODEL)
    lb_all = jnp.cumsum(jax.nn.softmax(hgrn_lb.astype(F32), axis=0), axis=0)
    lb_all = lb_all - lb_all[0:1]

    w_in_b = pack_w_in(w_in)
    w_out_b = w_out.astype(BF16)
    cmp_pos = jnp.stack([cmp_k_pos, cmp_v_pos], axis=1)
    cmp_w1 = jnp.stack([cmp_k_w1, cmp_v_w1], axis=1).astype(BF16)
    cmp_b1 = jnp.stack([cmp_k_b1, cmp_v_b1], axis=1).reshape(DEPTH, 2, 1, CMP_HIDDEN)
    cmp_w2 = jnp.stack([cmp_k_w2, cmp_v_w2], axis=1).astype(BF16)
    cmp_b2 = jnp.stack([cmp_k_b2, cmp_v_b2], axis=1).reshape(DEPTH, 2, 1, HEAD_DIM)
    wq_t = jnp.swapaxes(peer_wq, 1, 2).astype(BF16)
    keys = peer_keys.reshape(DEPTH, PEER_HEADS * 2, PEER_KEYS, PEER_KEYS)
    u_b = peer_u.astype(BF16)
    v_t = jnp.swapaxes(peer_v, 1, 2).astype(BF16)

    h = adaln(xf, norm_mix[0], mod, 0)
    for l in range(DEPTH):
        proj = in_proj(h, w_in_b, l)
        cmp_kv = compress(proj, cmp_pos, cmp_w1, cmp_b1, cmp_w2, cmp_b2, l)
        o_nsa = nsa_attention(proj, cmp_kv, rope_cs, rope_sn)
        o_hgrn = hgrn(proj, lb_all[l], hgrn_norm[l])
        x1, ht = out_proj(o_nsa, o_hgrn, w_out_b, xf, norm_ffn[l], mod, l)
        eb, ea, thr = peer_scores(ht, wq_t, keys, l)
        out_t = peer_dense(ht, eb, ea, thr, u_b, v_t, l)
        if l + 1 < DEPTH:
            xf, h = peer_residual(x1, out_t, mod, l, norm_mix[l + 1])
        else:
            xf = final_residual_norm(x1, out_t, mod, l, final_norm)
    return xf.reshape(BATCH, SEQ, D_MODEL)
```

```python
import functools

import numpy as np
import jax
import jax.numpy as jnp
from jax import lax
from jax.experimental import pallas as pl
from jax.experimental.pallas import tpu as pltpu

F32 = jnp.float32
BF16 = jnp.bfloat16

D_MODEL = 2048
BATCH = 4
SEQ = 2048
TOKENS = BATCH * SEQ
DEPTH = 2
EPS = 1e-6

HEAD_DIM = 128
NSA_HEADS = 8
NSA_GROUPS = 2
NSA_HPG = NSA_HEADS // NSA_GROUPS
CMP_BLOCK = 32
CMP_STRIDE = 16
CMP_HIDDEN = 256
N_CMP = (SEQ - CMP_BLOCK) // CMP_STRIDE + 1
N_CMP_PAD = 128
SLC_BLOCK = 64
N_SLC = SEQ // SLC_BLOCK
SLC_TOPN = 16
N_LOCAL_FORCED = 2
WINDOW = 512
ROT_HALF = 16
ROPE_THETA = 500000.0
Q_TILE = 128
WIN_KEYS = WINDOW + Q_TILE

HGRN_HEADS = 8
HGRN_CHUNK = 64
HGRN_SUB = 16

PEER_HEADS = 8
PEER_KEYS = 128
PEER_EXPERTS = PEER_KEYS * PEER_KEYS
PEER_TOPK = 16

IN_WIDTH_PACKED = 6912
COL_Q, COL_KC, COL_VC, COL_KS, COL_VS, COL_KW, COL_VW = 0, 8, 10, 12, 14, 16, 18
COL_HQ, COL_HF, COL_HI, COL_HG, COL_GL = 20, 28, 36, 44, 52

VMEM_LIMIT = 56 * 1024 * 1024


def _params(sem):
    return pltpu.CompilerParams(dimension_semantics=sem, vmem_limit_bytes=VMEM_LIMIT)


def _ada_kernel(c_ref, w_ref, b_ref, o_ref):
    c = c_ref[...]
    o_ref[0] = jnp.dot(c * jax.nn.sigmoid(c), w_ref[0], preferred_element_type=F32) + b_ref[0]


def ada_mod(c8, w_ada, b_ada):
    tn = 1024
    n = 6 * D_MODEL
    return pl.pallas_call(
        _ada_kernel,
        grid=(DEPTH, n // tn),
        in_specs=[
            pl.BlockSpec((8, D_MODEL), lambda l, j: (0, 0)),
            pl.BlockSpec((1, D_MODEL, tn), lambda l, j: (l, 0, j)),
            pl.BlockSpec((1, 1, tn), lambda l, j: (l, 0, j)),
        ],
        out_specs=pl.BlockSpec((1, 8, tn), lambda l, j: (l, 0, j)),
        out_shape=jax.ShapeDtypeStruct((DEPTH, 8, n), F32),
        compiler_params=_params(("parallel", "parallel")),
        name="ada_mod",
    )(c8, w_ada, b_ada.reshape(DEPTH, 1, n))


def _mod_spec(layer, k, rows_per_batch):
    return pl.BlockSpec((1, 1, D_MODEL), lambda i: ((layer * 8 + i // rows_per_batch) * 6 + k, 0, 0))


def _rms_mod(x, gain, scale, shift):
    y = x * lax.rsqrt(jnp.mean(x * x, axis=-1, keepdims=True) + EPS) * gain
    return y * (1.0 + scale) + shift


def _adaln_kernel(x_ref, gain_ref, sh_ref, sc_ref, h_ref):
    h_ref[...] = _rms_mod(x_ref[...], gain_ref[...], sc_ref[0], sh_ref[0]).astype(BF16)


def adaln(x, gain, mod, layer):
    tm = 256
    return pl.pallas_call(
        _adaln_kernel,
        grid=(TOKENS // tm,),
        in_specs=[
            pl.BlockSpec((tm, D_MODEL), lambda i: (i, 0)),
            pl.BlockSpec((1, D_MODEL), lambda i: (0, 0)),
            _mod_spec(layer, 0, SEQ // tm),
            _mod_spec(layer, 1, SEQ // tm),
        ],
        out_specs=pl.BlockSpec((tm, D_MODEL), lambda i: (i, 0)),
        out_shape=jax.ShapeDtypeStruct((TOKENS, D_MODEL), BF16),
        compiler_params=_params(("parallel",)),
        name="adaln",
    )(x, gain.reshape(1, D_MODEL), mod, mod)


def _matmul_kernel(a_ref, b_ref, o_ref):
    o_ref[...] = jnp.dot(a_ref[...], b_ref[...], preferred_element_type=F32)


def in_proj(h, w, layer):
    tm, tn = 1024, 768
    return pl.pallas_call(
        _matmul_kernel,
        grid=(TOKENS // tm, IN_WIDTH_PACKED // tn),
        in_specs=[
            pl.BlockSpec((tm, D_MODEL), lambda i, j: (i, 0)),
            pl.BlockSpec((None, D_MODEL, tn), lambda i, j: (layer, 0, j)),
        ],
        out_specs=pl.BlockSpec((tm, tn), lambda i, j: (i, j)),
        out_shape=jax.ShapeDtypeStruct((TOKENS, IN_WIDTH_PACKED), F32),
        compiler_params=_params(("parallel", "parallel")),
        name="in_proj",
    )(h, w)


def pack_w_in(w):
    gl = w[..., 2560:2584]
    z = jnp.zeros(w.shape[:-1] + (116,), w.dtype)
    return jnp.concatenate([w[..., :2560], w[..., 2584:], gl[..., :12], z, gl[..., 12:], z], axis=-1).astype(BF16)


def _compress_kernel(t_ref, pos_ref, w1_ref, b1_ref, w2_ref, b2_ref, o_ref, pad_ref):
    pad_ref[0:SEQ, :] = t_ref[...]
    pad_ref[SEQ:SEQ + CMP_BLOCK, :] = jnp.zeros((CMP_BLOCK, HEAD_DIM), F32)
    acc = jnp.zeros((N_CMP_PAD, CMP_HIDDEN), F32)
    for l in range(CMP_BLOCK):
        tl = pad_ref[pl.ds(l, N_CMP_PAD, stride=CMP_STRIDE), :] + pos_ref[0, l:l + 1, :]
        acc = acc + jnp.dot(tl.astype(BF16), w1_ref[0, l * HEAD_DIM:(l + 1) * HEAD_DIM, :],
                            preferred_element_type=F32)
    hid = jax.nn.gelu(acc + b1_ref[0])
    o_ref[0, 0] = jnp.dot(hid.astype(BF16), w2_ref[0], preferred_element_type=F32) + b2_ref[0]


def compress(proj, pos, w1, b1, w2, b2, layer):
    return pl.pallas_call(
        _compress_kernel,
        grid=(2, BATCH, NSA_GROUPS),
        in_specs=[
            pl.BlockSpec((SEQ, HEAD_DIM), lambda kv, b, g: (b, COL_KC + 2 * kv + g)),
            pl.BlockSpec((None, 1, CMP_BLOCK, HEAD_DIM), lambda kv, b, g: (layer, kv, 0, 0)),
            pl.BlockSpec((None, 1, CMP_BLOCK * HEAD_DIM, CMP_HIDDEN), lambda kv, b, g: (layer, kv, 0, 0)),
            pl.BlockSpec((None, 1, 1, CMP_HIDDEN), lambda kv, b, g: (layer, kv, 0, 0)),
            pl.BlockSpec((None, 1, CMP_HIDDEN, HEAD_DIM), lambda kv, b, g: (layer, kv, 0, 0)),
            pl.BlockSpec((None, 1, 1, HEAD_DIM), lambda kv, b, g: (layer, kv, 0, 0)),
        ],
        out_specs=pl.BlockSpec((1, 1, N_CMP_PAD, HEAD_DIM), lambda kv, b, g: (kv, b * NSA_GROUPS + g, 0, 0)),
        out_shape=jax.ShapeDtypeStruct((2, BATCH * NSA_GROUPS, N_CMP_PAD, HEAD_DIM), F32),
        scratch_shapes=[pltpu.VMEM((SEQ + CMP_BLOCK, HEAD_DIM), F32)],
        compiler_params=_params(("arbitrary", "arbitrary", "arbitrary")),
        name="nsa_compress",
    )(proj, pos, w1, b1, w2, b2)


def _rope(t, cs, sn):
    lane = lax.broadcasted_iota(jnp.int32, t.shape, 1)
    swapped = jnp.where(lane < ROT_HALF, pltpu.roll(t, HEAD_DIM - ROT_HALF, axis=1), pltpu.roll(t, ROT_HALF, axis=1))
    return t * cs + swapped * sn


def _dot_nt(a, b):
    return lax.dot_general(a, b, (((1,), (1,)), ((), ())), preferred_element_type=F32)


MASKED = -1e30
GROUP_TILES = 2


def _nsa_kernel(q_ref, kc_ref, vc_ref, ks_ref, vs_ref, kw_ref, vw_ref, gl_ref, cs_ref, sn_ref, wov_ref,
                o_ref, kaug, vaug, kwr, vwaug):
    scale = HEAD_DIM ** -0.5
    neg_inf = -jnp.inf
    row = lax.broadcasted_iota(jnp.int32, (SEQ, HEAD_DIM), 0)
    lane = lax.broadcasted_iota(jnp.int32, (SEQ, HEAD_DIM), 1)
    kaug[:, 0:HEAD_DIM] = _rope(ks_ref[...], cs_ref[...], sn_ref[...]).astype(BF16)
    kaug[:, HEAD_DIM:2 * HEAD_DIM] = jnp.where(lane == (row >> 6), 1.0, 0.0).astype(BF16)
    kwr[...] = _rope(kw_ref[...], cs_ref[...], sn_ref[...]).astype(BF16)
    ones = jnp.ones((SEQ, HEAD_DIM), BF16)
    vaug[:, 0:HEAD_DIM] = vs_ref[...].astype(BF16)
    vaug[:, HEAD_DIM:2 * HEAD_DIM] = ones
    vwaug[:, 0:HEAD_DIM] = vw_ref[...].astype(BF16)
    vwaug[:, HEAD_DIM:2 * HEAD_DIM] = ones
    kcb = kc_ref[0, 0].astype(BF16)
    vcb = vc_ref[0, 0].astype(BF16)

    def tile(i, extent):
        t0 = pl.multiple_of(i * Q_TILE, Q_TILE)
        q4 = q_ref[pl.ds(t0, Q_TILE), :] * scale
        cs = cs_ref[pl.ds(t0, Q_TILE), :]
        sn = sn_ref[pl.ds(t0, Q_TILE), :]
        gate = jax.nn.sigmoid(gl_ref[pl.ds(t0, Q_TILE), :])
        qh = [q4[:, h * HEAD_DIM:(h + 1) * HEAD_DIM] for h in range(NSA_HPG)]

        qs = jnp.concatenate(qh, axis=0).astype(BF16)
        sc = _dot_nt(qs, kcb)
        rows = lax.broadcasted_iota(jnp.int32, sc.shape, 0)
        ncol = lax.broadcasted_iota(jnp.int32, sc.shape, 1)
        tok = t0 + (rows & (Q_TILE - 1))
        valid = ncol * CMP_STRIDE + (CMP_BLOCK - 1) <= tok
        s = jnp.where(valid, sc, neg_inf)
        m = jnp.max(s, axis=-1, keepdims=True)
        m = jnp.where(m == neg_inf, 0.0, m)
        e = jnp.where(valid, jnp.exp(s - m), 0.0)
        p = e / jnp.maximum(jnp.sum(e, axis=-1, keepdims=True), 1e-30)
        o_cmp = jnp.dot(p.astype(BF16), vcb, preferred_element_type=F32)

        p4 = p[0:Q_TILE] + p[Q_TILE:2 * Q_TILE] + p[2 * Q_TILE:3 * Q_TILE] + p[3 * Q_TILE:4 * Q_TILE]
        imp = jnp.dot(p4, wov_ref[...], preferred_element_type=F32)
        jj = lax.broadcasted_iota(jnp.int32, imp.shape, 1)
        tt = t0 + lax.broadcasted_iota(jnp.int32, imp.shape, 0)
        back = (tt >> 6) - jj
        forced = (jj == 0) | (back.astype(jnp.uint32) < N_LOCAL_FORCED)
        score = jnp.where(forced, jnp.inf, jnp.where(jj * SLC_BLOCK <= tt, imp, neg_inf))
        score_t = score.T[0:N_SLC]
        jrow = lax.broadcasted_iota(jnp.int32, score_t.shape, 0)
        rank = jnp.zeros(score_t.shape, F32)
        for jp in range(N_SLC):
            other = score_t[jp:jp + 1]
            later = jnp.where(jrow > jp, 1.0, 0.0)
            rank = rank + jnp.where(other > score_t, 1.0, jnp.where(other == score_t, later, 0.0))
        bias_t = jnp.where(rank < SLC_TOPN, 0.0, MASKED)
        sel_bias = jnp.concatenate([bias_t, jnp.zeros((HEAD_DIM - N_SLC, Q_TILE), F32)], axis=0).T

        tail = GROUP_TILES * Q_TILE
        rows4 = NSA_HPG * Q_TILE
        kpos = (extent - tail) + lax.broadcasted_iota(jnp.int32, (rows4, tail), 1)
        ok_tail = kpos <= t0 + (lax.broadcasted_iota(jnp.int32, (rows4, tail), 0) & (Q_TILE - 1))

        qr = jnp.concatenate([_rope(qh[h], cs, sn) for h in range(NSA_HPG)], axis=0)
        qa = jnp.concatenate([qr, jnp.concatenate([sel_bias] * NSA_HPG, axis=0)], axis=1).astype(BF16)
        s1 = _dot_nt(qa, kaug[0:extent, :])
        s1_tail = jnp.where(ok_tail, s1[:, extent - tail:], MASKED)
        s1 = s1_tail if extent == tail else jnp.concatenate([s1[:, :extent - tail], s1_tail], axis=1)
        e1 = jnp.exp(s1 - jnp.max(s1, axis=-1, keepdims=True)).astype(BF16)
        r1 = jnp.dot(e1, vaug[0:extent, :], preferred_element_type=F32)
        o_slc = r1[:, :HEAD_DIM] / r1[:, HEAD_DIM:HEAD_DIM + 1]

        ws = pl.multiple_of(jnp.clip(t0 - WINDOW, 0, SEQ - WIN_KEYS), Q_TILE)
        wk = ws + lax.broadcasted_iota(jnp.int32, (rows4, WIN_KEYS), 1)
        wt = t0 + (lax.broadcasted_iota(jnp.int32, (rows4, WIN_KEYS), 0) & (Q_TILE - 1))
        ok_win = (wt - wk).astype(jnp.uint32) < WINDOW
        s2 = jnp.where(ok_win, _dot_nt(qr.astype(BF16), kwr[pl.ds(ws, WIN_KEYS), :]), MASKED)
        e2 = jnp.exp(s2 - jnp.max(s2, axis=-1, keepdims=True)).astype(BF16)
        r2 = jnp.dot(e2, vwaug[pl.ds(ws, WIN_KEYS), :], preferred_element_type=F32)
        o_win = r2[:, :HEAD_DIM] / r2[:, HEAD_DIM:HEAD_DIM + 1]

        outs = []
        for h in range(NSA_HPG):
            hr = slice(h * Q_TILE, (h + 1) * Q_TILE)
            outs.append(gate[:, 3 * h:3 * h + 1] * o_cmp[hr] + gate[:, 3 * h + 1:3 * h + 2] * o_slc[hr]
                        + gate[:, 3 * h + 2:3 * h + 3] * o_win[hr])
        o_ref[pl.ds(t0, Q_TILE), :] = jnp.concatenate(outs, axis=1).astype(BF16)

    for grp in range(SEQ // (GROUP_TILES * Q_TILE)):
        extent = (grp + 1) * GROUP_TILES * Q_TILE

        def body(ii, carry, grp=grp, extent=extent):
            tile(grp * GROUP_TILES + ii, extent)
            return carry

        lax.fori_loop(0, GROUP_TILES, body, 0)


def _cmp_to_slc_weights():
    cs = np.arange(N_CMP) * CMP_STRIDE
    ce = cs + CMP_BLOCK
    ss = np.arange(N_SLC) * SLC_BLOCK
    se = ss + SLC_BLOCK
    ov = np.clip(np.minimum(ce[:, None], se[None, :]) - np.maximum(cs[:, None], ss[None, :]), 0, None)
    w = np.zeros((N_CMP_PAD, HEAD_DIM), np.float32)
    w[:N_CMP, :N_SLC] = ov / CMP_BLOCK
    return w


def nsa_attention(proj, cmp_kv, rope_cs, rope_sn):
    gw = NSA_HPG * HEAD_DIM
    col = lambda c: pl.BlockSpec((SEQ, HEAD_DIM), lambda b, g: (b, c + g))
    return pl.pallas_call(
        _nsa_kernel,
        grid=(BATCH, NSA_GROUPS),
        in_specs=[
            pl.BlockSpec((SEQ, gw), lambda b, g: (b, g)),
            pl.BlockSpec((1, 1, N_CMP_PAD, HEAD_DIM), lambda b, g: (0, b * NSA_GROUPS + g, 0, 0)),
            pl.BlockSpec((1, 1, N_CMP_PAD, HEAD_DIM), lambda b, g: (1, b * NSA_GROUPS + g, 0, 0)),
            col(COL_KS), col(COL_VS), col(COL_KW), col(COL_VW), col(COL_GL),
            pl.BlockSpec((SEQ, HEAD_DIM), lambda b, g: (b, 0)),
            pl.BlockSpec((SEQ, HEAD_DIM), lambda b, g: (b, 0)),
            pl.BlockSpec((N_CMP_PAD, HEAD_DIM), lambda b, g: (0, 0)),
        ],
        out_specs=pl.BlockSpec((SEQ, gw), lambda b, g: (b, g)),
        out_shape=jax.ShapeDtypeStruct((TOKENS, NSA_HEADS * HEAD_DIM), BF16),
        scratch_shapes=[pltpu.VMEM((SEQ, 2 * HEAD_DIM), BF16), pltpu.VMEM((SEQ, 2 * HEAD_DIM), BF16),
                        pltpu.VMEM((SEQ, HEAD_DIM), BF16), pltpu.VMEM((SEQ, 2 * HEAD_DIM), BF16)],
        compiler_params=_params(("parallel", "parallel")),
        name="nsa_attention",
    )(proj, cmp_kv, cmp_kv, proj, proj, proj, proj, proj, rope_cs, rope_sn, jnp.asarray(_cmp_to_slc_weights()))


HGRN_HEADS_PER_STEP = 4


def _hgrn_chunk(qc, fc, vc, gc, lb, log_lb, log_1m, gn, tri, state_t):
    C, SUB = HGRN_CHUNK, HGRN_SUB
    row_sub = lax.broadcasted_iota(jnp.int32, (SUB, HEAD_DIM), 0)
    qh = qc * jax.nn.sigmoid(qc)
    log_sig = jnp.minimum(fc, 0.0) - jnp.log(1.0 + jnp.exp(-jnp.abs(fc)))
    b = log_1m + log_sig
    lf = jnp.maximum(log_lb, b) + jnp.log(1.0 + jnp.exp(-jnp.abs(log_lb - b)))
    kh = (1.0 - lb) * jax.nn.sigmoid(-fc)
    a = jnp.dot(tri, lf, preferred_element_type=F32)
    inter = _dot_nt(qh * jnp.exp(a), state_t)
    parts = []
    for blk in range(C // SUB):
        lo = blk * SUB
        ab, qb, kb, vb = a[lo:lo + SUB], qh[lo:lo + SUB], kh[lo:lo + SUB], vc[lo:lo + SUB]
        acc = inter[lo:lo + SUB]
        if blk > 0:
            a_ref_row = a[lo - 1:lo]
            qt = qb * jnp.exp(ab - a_ref_row)
            kt = kh[0:lo] * jnp.exp(a_ref_row - a[0:lo])
            acc = acc + jnp.dot(_dot_nt(qt, kt), vc[0:lo], preferred_element_type=F32)
        for s in range(SUB):
            dec = jnp.exp(jnp.where(row_sub >= s, ab - ab[s:s + 1], -jnp.inf))
            w = jnp.sum(qb * kb[s:s + 1] * dec, axis=-1, keepdims=True)
            acc = acc + w * vb[s:s + 1]
        parts.append(acc)
    o = jnp.concatenate(parts, axis=0)
    a_end = a[C - 1:C]
    kt_end = kh * jnp.exp(a_end - a)
    state_t = state_t * jnp.exp(a_end) + lax.dot_general(vc, kt_end, (((0,), (0,)), ((), ())),
                                                          preferred_element_type=F32)
    o = o * lax.rsqrt(jnp.mean(o * o, axis=-1, keepdims=True) + EPS) * gn
    return (o * jax.nn.sigmoid(gc)).astype(BF16), state_t


def _hgrn_kernel(q_ref, f_ref, i_ref, g_ref, lb_ref, gn_ref, tri_ref, o_ref):
    C, HP = HGRN_CHUNK, HGRN_HEADS_PER_STEP
    lbs = [lb_ref[hp] for hp in range(HP)]
    log_lbs = [jnp.log(lb) for lb in lbs]
    log_1ms = [jnp.log(1.0 - lb) for lb in lbs]
    gns = [gn_ref[hp] for hp in range(HP)]

    def chunk(c, states):
        r0 = pl.multiple_of(c * C, C)
        qc = q_ref[pl.ds(r0, C), :]
        fc = f_ref[pl.ds(r0, C), :]
        vc = i_ref[pl.ds(r0, C), :]
        gc = g_ref[pl.ds(r0, C), :]
        outs, new_states = [], []
        for hp in range(HP):
            cols = slice(hp * HEAD_DIM, (hp + 1) * HEAD_DIM)
            o, st = _hgrn_chunk(qc[:, cols], fc[:, cols], vc[:, cols], gc[:, cols], lbs[hp], log_lbs[hp], log_1ms[hp],
                                gns[hp], tri_ref[...], states[hp])
            outs.append(o)
            new_states.append(st)
        o_ref[pl.ds(r0, C), :] = jnp.concatenate(outs, axis=1)
        return tuple(new_states)

    lax.fori_loop(0, SEQ // C, chunk, tuple(jnp.zeros((HEAD_DIM, HEAD_DIM), F32) for _ in range(HP)))


def hgrn(proj, lb, gain):
    hp = HGRN_HEADS_PER_STEP
    col = lambda c: pl.BlockSpec((SEQ, hp * HEAD_DIM), lambda b, h: (b, c // hp + h))
    vec = pl.BlockSpec((hp, 1, HEAD_DIM), lambda b, h: (h, 0, 0))
    tri = np.tril(np.ones((HGRN_CHUNK, HGRN_CHUNK), np.float32))
    return pl.pallas_call(
        _hgrn_kernel,
        grid=(BATCH, HGRN_HEADS // hp),
        in_specs=[col(COL_HQ), col(COL_HF), col(COL_HI), col(COL_HG), vec, vec,
                  pl.BlockSpec((HGRN_CHUNK, HGRN_CHUNK), lambda b, h: (0, 0))],
        out_specs=pl.BlockSpec((SEQ, hp * HEAD_DIM), lambda b, h: (b, h)),
        out_shape=jax.ShapeDtypeStruct((TOKENS, HGRN_HEADS * HEAD_DIM), BF16),
        compiler_params=_params(("parallel", "parallel")),
        name="hgrn2",
    )(proj, proj, proj, proj, lb.reshape(HGRN_HEADS, 1, HEAD_DIM), gain.reshape(HGRN_HEADS, 1, HEAD_DIM),
      jnp.asarray(tri))


def _out_proj_kernel(on_ref, oh_ref, wn_ref, wh_ref, x_ref, g_ref, gain_ref, sh_ref, sc_ref, x1_ref, ht_ref):
    mix = (jnp.dot(on_ref[...], wn_ref[...], preferred_element_type=F32)
           + jnp.dot(oh_ref[...], wh_ref[...], preferred_element_type=F32))
    x1 = x_ref[...] + g_ref[0] * mix
    x1_ref[...] = x1
    ht_ref[...] = _rms_mod(x1, gain_ref[...], sc_ref[0], sh_ref[0]).T.astype(BF16)


def out_proj(o_nsa, o_hgrn, w_out, x, gain, mod, layer):
    tm = 256
    half = D_MODEL // 2
    rows = pl.BlockSpec((tm, half), lambda i: (i, 0))
    full = pl.BlockSpec((tm, D_MODEL), lambda i: (i, 0))
    return pl.pallas_call(
        _out_proj_kernel,
        grid=(TOKENS // tm,),
        in_specs=[rows, rows,
                  pl.BlockSpec((None, half, D_MODEL), lambda i: (layer, 0, 0)),
                  pl.BlockSpec((None, half, D_MODEL), lambda i: (layer, 1, 0)),
                  full, _mod_spec(layer, 2, SEQ // tm),
                  pl.BlockSpec((1, D_MODEL), lambda i: (0, 0)),
                  _mod_spec(layer, 3, SEQ // tm), _mod_spec(layer, 4, SEQ // tm)],
        out_specs=[full, pl.BlockSpec((D_MODEL, tm), lambda i: (0, i))],
        out_shape=[jax.ShapeDtypeStruct((TOKENS, D_MODEL), F32), jax.ShapeDtypeStruct((D_MODEL, TOKENS), BF16)],
        compiler_params=_params(("parallel",)),
        name="out_proj",
    )(o_nsa, o_hgrn, w_out, w_out, x, mod, gain.reshape(1, D_MODEL), mod, mod)


NOT_TOP = 127.0


def _pack_pairs(x):
    return pltpu.bitcast(x, jnp.uint32)


def _unpack_pairs(x):
    return pltpu.bitcast(x, BF16)


def _top_values(x, k, want_rank=False):
    vals = []
    rank = jnp.full(x.shape, NOT_TOP, F32) if want_rank else None
    for i in range(k):
        m = jnp.max(x, axis=0, keepdims=True)
        vals.append(m)
        hit = x == m
        if want_rank:
            rank = jnp.where(hit, float(i), rank)
        x = jnp.where(hit, -jnp.inf, x)
    return vals, rank


def _dot_tn(a, b):
    return lax.dot_general(a, b, (((0,), (0,)), ((), ())), preferred_element_type=F32)


def _peer_score_kernel(ht_ref, wq_ref, keys_ref, eb_ref, rb_ref, ea_ref, cnt_ref):
    k = PEER_TOPK
    qt = _dot_tn(wq_ref[...], ht_ref[...])
    for h in range(PEER_HEADS):
        sa = jnp.dot(keys_ref[2 * h], qt[(2 * h) * PEER_KEYS:(2 * h + 1) * PEER_KEYS], preferred_element_type=F32)
        sb = jnp.dot(keys_ref[2 * h + 1], qt[(2 * h + 1) * PEER_KEYS:(2 * h + 2) * PEER_KEYS],
                     preferred_element_type=F32)
        top_a, _ = _top_values(sa, k)
        top_b, rank_b = _top_values(sb, k, want_rank=True)
        a16 = jnp.concatenate(top_a, axis=0)
        b16 = jnp.concatenate(top_b, axis=0)
        cand = jnp.concatenate([a16 + top_b[0]] + [a16[0:8] + top_b[j] for j in range(1, 8)]
                               + [top_a[0] + b16[8:16]], axis=0)
        best, _ = _top_values(cand, k)
        z = jnp.zeros_like(best[0])
        for v in best:
            z = z + jnp.exp(v - best[0])
        cnt = jnp.zeros_like(sa)
        for j in range(k):
            cnt = cnt + jnp.where(sa + top_b[j] >= best[k - 1], 1.0, 0.0)
        half = slice(h * PEER_KEYS // 2, (h + 1) * PEER_KEYS // 2)
        eb_ref[half, :] = _pack_pairs(jnp.exp(sb - top_b[0]).astype(BF16))
        rb_ref[half, :] = _pack_pairs(rank_b.astype(BF16))
        ea_ref[h] = jnp.exp(sa - top_a[0]) / z
        cnt_ref[h] = cnt


def peer_scores(ht, wq, keys, layer):
    tm = 256
    rows = PEER_HEADS * PEER_KEYS
    spec2 = pl.BlockSpec((rows // 2, tm), lambda i: (0, i))
    shape2 = jax.ShapeDtypeStruct((rows // 2, TOKENS), jnp.uint32)
    spec3 = pl.BlockSpec((PEER_HEADS, PEER_KEYS, tm), lambda i: (0, 0, i))
    shape3 = jax.ShapeDtypeStruct((PEER_HEADS, PEER_KEYS, TOKENS), F32)
    return pl.pallas_call(
        _peer_score_kernel,
        grid=(TOKENS // tm,),
        in_specs=[pl.BlockSpec((D_MODEL, tm), lambda i: (0, i)),
                  pl.BlockSpec((None, D_MODEL, 2 * rows), lambda i: (layer, 0, 0)),
                  pl.BlockSpec((None, PEER_HEADS * 2, PEER_KEYS, PEER_KEYS), lambda i: (layer, 0, 0, 0))],
        out_specs=[spec2, spec2, spec3, spec3],
        out_shape=[shape2, shape2, shape3, shape3],
        compiler_params=_params(("parallel",)),
        name="peer_scores",
    )(ht, wq, keys)


PEER_TM = 512
PEER_TE = 1024
PEER_LANES = 128


def _peer_dense_kernel(xt_ref, eb_ref, rb_ref, ea_ref, cnt_ref, u_ref, v_ref, o_ref, gw_ref):
    @pl.when(pl.program_id(1) == 0)
    def _():
        o_ref[...] = jnp.zeros_like(o_ref)

    def rows_bf16(ref, h, ab, cols):
        tile16 = jnp.broadcast_to(ref[h, ab:ab + 1, cols], (16, PEER_LANES)).astype(BF16)
        return jnp.concatenate([tile16] * (PEER_KEYS // 16), axis=0)

    act = jnp.dot(u_ref[...], xt_ref[...], preferred_element_type=F32)
    for ab in range(PEER_TE // PEER_KEYS):
        rows = slice(ab * PEER_KEYS, (ab + 1) * PEER_KEYS)
        for lt in range(PEER_TM // PEER_LANES):
            cols = slice(lt * PEER_LANES, (lt + 1) * PEER_LANES)
            w = jnp.zeros((PEER_KEYS, PEER_LANES), BF16)
            for h in range(PEER_HEADS):
                hb = slice(h * PEER_KEYS // 2, (h + 1) * PEER_KEYS // 2)
                eb = _unpack_pairs(eb_ref[hb, cols])
                rb = _unpack_pairs(rb_ref[hb, cols])
                cnt = rows_bf16(cnt_ref, h, ab, cols)
                w = w + jnp.where(rb < cnt, eb * rows_bf16(ea_ref, h, ab, cols), 0.0)
            gw = jax.nn.gelu(act[rows, cols].astype(BF16)) * w
            gw_ref[rows, cols] = gw
    o_ref[...] += _dot_tn(v_ref[...], gw_ref[...])


def peer_dense(ht, eb, rb, ea, cnt, u, v, layer):
    ab = PEER_TE // PEER_KEYS
    rows = pl.BlockSpec((PEER_HEADS, ab, PEER_TM), lambda i, j: (0, j, i))
    half = pl.BlockSpec((PEER_HEADS * PEER_KEYS // 2, PEER_TM), lambda i, j: (0, i))
    return pl.pallas_call(
        _peer_dense_kernel,
        grid=(TOKENS // PEER_TM, PEER_EXPERTS // PEER_TE),
        in_specs=[pl.BlockSpec((D_MODEL, PEER_TM), lambda i, j: (0, i)),
                  half, half, rows, rows,
                  pl.BlockSpec((None, PEER_TE, D_MODEL), lambda i, j: (layer, j, 0)),
                  pl.BlockSpec((None, PEER_TE, D_MODEL), lambda i, j: (layer, j, 0))],
        out_specs=pl.BlockSpec((D_MODEL, PEER_TM), lambda i, j: (0, i)),
        out_shape=jax.ShapeDtypeStruct((D_MODEL, TOKENS), F32),
        scratch_shapes=[pltpu.VMEM((PEER_TE, PEER_TM), BF16)],
        compiler_params=_params(("parallel", "arbitrary")),
        name="peer_dense",
    )(ht, eb, rb, ea, cnt, u, v)


def _peer_resid_kernel(x_ref, ot_ref, g_ref, gain_ref, sh_ref, sc_ref, x2_ref, h_ref):
    x2 = x_ref[...] + g_ref[0] * ot_ref[...].T
    x2_ref[...] = x2
    h_ref[...] = _rms_mod(x2, gain_ref[...], sc_ref[0], sh_ref[0]).astype(BF16)


def _final_kernel(x_ref, ot_ref, g_ref, gain_ref, o_ref):
    x2 = x_ref[...] + g_ref[0] * ot_ref[...].T
    o_ref[...] = x2 * lax.rsqrt(jnp.mean(x2 * x2, axis=-1, keepdims=True) + EPS) * gain_ref[...]


def peer_residual(x1, out_t, mod, layer, gain_next):
    tm = 256
    full = pl.BlockSpec((tm, D_MODEL), lambda i: (i, 0))
    return pl.pallas_call(
        _peer_resid_kernel,
        grid=(TOKENS // tm,),
        in_specs=[full, pl.BlockSpec((D_MODEL, tm), lambda i: (0, i)), _mod_spec(layer, 5, SEQ // tm),
                  pl.BlockSpec((1, D_MODEL), lambda i: (0, 0)),
                  _mod_spec(layer + 1, 0, SEQ // tm), _mod_spec(layer + 1, 1, SEQ // tm)],
        out_specs=[full, full],
        out_shape=[jax.ShapeDtypeStruct((TOKENS, D_MODEL), F32), jax.ShapeDtypeStruct((TOKENS, D_MODEL), BF16)],
        compiler_params=_params(("parallel",)),
        name="peer_residual",
    )(x1, out_t, mod, gain_next.reshape(1, D_MODEL), mod, mod)


def final_residual_norm(x1, out_t, mod, layer, gain):
    tm = 256
    full = pl.BlockSpec((tm, D_MODEL), lambda i: (i, 0))
    return pl.pallas_call(
        _final_kernel,
        grid=(TOKENS // tm,),
        in_specs=[full, pl.BlockSpec((D_MODEL, tm), lambda i: (0, i)), _mod_spec(layer, 5, SEQ // tm),
                  pl.BlockSpec((1, D_MODEL), lambda i: (0, 0))],
        out_specs=full,
        out_shape=jax.ShapeDtypeStruct((TOKENS, D_MODEL), F32),
        compiler_params=_params(("parallel",)),
        name="final_norm",
    )(x1, out_t, mod, gain.reshape(1, D_MODEL))


def _rope_tables(positions):
    inv = ROPE_THETA ** (-jnp.arange(0, 2 * ROT_HALF, 2, dtype=F32) / (2 * ROT_HALF))
    ang = positions.astype(F32).reshape(TOKENS, 1) * inv
    cos, sin = jnp.cos(ang), jnp.sin(ang)
    rest = HEAD_DIM - 2 * ROT_HALF
    cs = jnp.concatenate([cos, cos, jnp.ones((TOKENS, rest), F32)], axis=1)
    sn = jnp.concatenate([-sin, sin, jnp.zeros((TOKENS, rest), F32)], axis=1)
    return cs, sn


def kernel(x, c, positions, norm_mix, norm_ffn, w_ada, b_ada, w_in, w_out, cmp_k_pos, cmp_k_w1, cmp_k_b1, cmp_k_w2, cmp_k_b2, cmp_v_pos, cmp_v_w1, cmp_v_b1, cmp_v_w2, cmp_v_b2, hgrn_norm, hgrn_lb, peer_wq, peer_keys, peer_u, peer_v, final_norm):
    xf = x.reshape(TOKENS, D_MODEL)
    rope_cs, rope_sn = _rope_tables(positions)
    c8 = jnp.concatenate([c, jnp.zeros((8 - BATCH, D_MODEL), F32)], axis=0)
    mod = ada_mod(c8, w_ada, b_ada).reshape(DEPTH * 8 * 6, 1, D_MODEL)
    lb_all = jnp.cumsum(jax.nn.softmax(hgrn_lb.astype(F32), axis=0), axis=0)
    lb_all = lb_all - lb_all[0:1]

    w_in_b = pack_w_in(w_in)
    w_out_b = w_out.astype(BF16)
    cmp_pos = jnp.stack([cmp_k_pos, cmp_v_pos], axis=1)
    cmp_w1 = jnp.stack([cmp_k_w1, cmp_v_w1], axis=1).astype(BF16)
    cmp_b1 = jnp.stack([cmp_k_b1, cmp_v_b1], axis=1).reshape(DEPTH, 2, 1, CMP_HIDDEN)
    cmp_w2 = jnp.stack([cmp_k_w2, cmp_v_w2], axis=1).astype(BF16)
    cmp_b2 = jnp.stack([cmp_k_b2, cmp_v_b2], axis=1).reshape(DEPTH, 2, 1, HEAD_DIM)
    wq_b = peer_wq.astype(BF16)
    keys = peer_keys.reshape(DEPTH, PEER_HEADS * 2, PEER_KEYS, PEER_KEYS)
    u_b = peer_u.astype(BF16)
    v_b = peer_v.astype(BF16)

    h = adaln(xf, norm_mix[0], mod, 0)
    for l in range(DEPTH):
        proj = in_proj(h, w_in_b, l)
        cmp_kv = compress(proj, cmp_pos, cmp_w1, cmp_b1, cmp_w2, cmp_b2, l)
        o_nsa = nsa_attention(proj, cmp_kv, rope_cs, rope_sn)
        o_hgrn = hgrn(proj, lb_all[l], hgrn_norm[l])
        x1, ht = out_proj(o_nsa, o_hgrn, w_out_b, xf, norm_ffn[l], mod, l)
        eb, rb, ea, cnt = peer_scores(ht, wq_b, keys, l)
        out_t = peer_dense(ht, eb, rb, ea, cnt, u_b, v_b, l)
        if l + 1 < DEPTH:
            xf, h = peer_residual(x1, out_t, mod, l, norm_mix[l + 1])
        else:
            xf = final_residual_norm(x1, out_t, mod, l, final_norm)
    return xf.reshape(BATCH, SEQ, D_MODEL)
```

```python
import functools

import numpy as np
import jax
import jax.numpy as jnp
from jax import lax
from jax.experimental import pallas as pl
from jax.experimental.pallas import tpu as pltpu

F32 = jnp.float32
BF16 = jnp.bfloat16

D_MODEL = 2048
BATCH = 4
SEQ = 2048
TOKENS = BATCH * SEQ
DEPTH = 2
EPS = 1e-6

HEAD_DIM = 128
NSA_HEADS = 8
NSA_GROUPS = 2
NSA_HPG = NSA_HEADS // NSA_GROUPS
CMP_BLOCK = 32
CMP_STRIDE = 16
CMP_HIDDEN = 256
N_CMP = (SEQ - CMP_BLOCK) // CMP_STRIDE + 1
N_CMP_PAD = 128
SLC_BLOCK = 64
N_SLC = SEQ // SLC_BLOCK
SLC_TOPN = 16
N_LOCAL_FORCED = 2
WINDOW = 512
ROT_HALF = 16
ROPE_THETA = 500000.0
Q_TILE = 128
WIN_KEYS = WINDOW + Q_TILE

HGRN_HEADS = 8
HGRN_CHUNK = 64
HGRN_SUB = 16

PEER_HEADS = 8
PEER_KEYS = 128
PEER_EXPERTS = PEER_KEYS * PEER_KEYS
PEER_TOPK = 16

NSA_COLS = 2560
COL_Q, COL_KC, COL_VC, COL_KS, COL_VS, COL_KW, COL_VW = 0, 8, 10, 12, 14, 16, 18
COL_HQ, COL_HF, COL_HI, COL_HG = 0, 8, 16, 24

VMEM_LIMIT = 60 * 1024 * 1024


def _params(sem):
    return pltpu.CompilerParams(dimension_semantics=sem, vmem_limit_bytes=VMEM_LIMIT)


def _ada_kernel(c_ref, w_ref, b_ref, o_ref):
    c = c_ref[...]
    o_ref[0] = jnp.dot(c * jax.nn.sigmoid(c), w_ref[0], preferred_element_type=F32) + b_ref[0]


def ada_mod(c8, w_ada, b_ada):
    tn = 1024
    n = 6 * D_MODEL
    return pl.pallas_call(
        _ada_kernel,
        grid=(DEPTH, n // tn),
        in_specs=[
            pl.BlockSpec((8, D_MODEL), lambda l, j: (0, 0)),
            pl.BlockSpec((1, D_MODEL, tn), lambda l, j: (l, 0, j)),
            pl.BlockSpec((1, 1, tn), lambda l, j: (l, 0, j)),
        ],
        out_specs=pl.BlockSpec((1, 8, tn), lambda l, j: (l, 0, j)),
        out_shape=jax.ShapeDtypeStruct((DEPTH, 8, n), F32),
        compiler_params=_params(("parallel", "parallel")),
        name="ada_mod",
    )(c8, w_ada, b_ada.reshape(DEPTH, 1, n))


def _mod_spec(layer, k, rows_per_batch):
    return pl.BlockSpec((1, 1, D_MODEL), lambda i: ((layer * 8 + i // rows_per_batch) * 6 + k, 0, 0))


def _rms_mod(x, gain, scale, shift):
    y = x * lax.rsqrt(jnp.mean(x * x, axis=-1, keepdims=True) + EPS) * gain
    return y * (1.0 + scale) + shift


def _adaln_kernel(x_ref, gain_ref, sh_ref, sc_ref, h_ref):
    h_ref[...] = _rms_mod(x_ref[...], gain_ref[...], sc_ref[0], sh_ref[0]).astype(BF16)


def adaln(x, gain, mod, layer):
    tm = 256
    return pl.pallas_call(
        _adaln_kernel,
        grid=(TOKENS // tm,),
        in_specs=[
            pl.BlockSpec((tm, D_MODEL), lambda i: (i, 0)),
            pl.BlockSpec((1, D_MODEL), lambda i: (0, 0)),
            _mod_spec(layer, 0, SEQ // tm),
            _mod_spec(layer, 1, SEQ // tm),
        ],
        out_specs=pl.BlockSpec((tm, D_MODEL), lambda i: (i, 0)),
        out_shape=jax.ShapeDtypeStruct((TOKENS, D_MODEL), BF16),
        compiler_params=_params(("parallel",)),
        name="adaln",
    )(x, gain.reshape(1, D_MODEL), mod, mod)


def _matmul_nt_kernel(a_ref, bt_ref, o_ref):
    o_ref[...] = lax.dot_general(a_ref[...], bt_ref[...], (((1,), (1,)), ((), ())), preferred_element_type=F32)


def in_proj(h, wt, layer):
    n = wt.shape[1]
    tm, tn = 1024, min(n, 512)
    return pl.pallas_call(
        _matmul_nt_kernel,
        grid=(TOKENS // tm, n // tn),
        in_specs=[
            pl.BlockSpec((tm, D_MODEL), lambda i, j: (i, 0)),
            pl.BlockSpec((None, tn, D_MODEL), lambda i, j: (layer, j, 0)),
        ],
        out_specs=pl.BlockSpec((tm, tn), lambda i, j: (i, j)),
        out_shape=jax.ShapeDtypeStruct((TOKENS, n), F32),
        compiler_params=_params(("parallel", "parallel")),
        name="in_proj",
    )(h, wt)


def split_w_in(w):
    wt = jnp.swapaxes(w, 1, 2)
    gl = wt[:, NSA_COLS:NSA_COLS + 24]
    z = jnp.zeros((w.shape[0], 116, D_MODEL), w.dtype)
    gates = jnp.concatenate([gl[:, :12], z, gl[:, 12:], z], axis=1)
    return wt[:, :NSA_COLS].astype(BF16), wt[:, NSA_COLS + 24:].astype(BF16), gates.astype(BF16)


def _compress_kernel(t_ref, pos_ref, w1_ref, b1_ref, w2_ref, b2_ref, o_ref, pad_ref):
    pad_ref[0:SEQ, :] = t_ref[...]
    pad_ref[SEQ:SEQ + CMP_BLOCK, :] = jnp.zeros((CMP_BLOCK, HEAD_DIM), F32)
    acc = jnp.zeros((N_CMP_PAD, CMP_HIDDEN), F32)
    for l in range(CMP_BLOCK):
        tl = pad_ref[pl.ds(l, N_CMP_PAD, stride=CMP_STRIDE), :] + pos_ref[0, l:l + 1, :]
        acc = acc + jnp.dot(tl.astype(BF16), w1_ref[0, l * HEAD_DIM:(l + 1) * HEAD_DIM, :],
                            preferred_element_type=F32)
    hid = jax.nn.gelu(acc + b1_ref[0])
    o_ref[0, 0] = jnp.dot(hid.astype(BF16), w2_ref[0], preferred_element_type=F32) + b2_ref[0]


def compress(proj, pos, w1, b1, w2, b2, layer):
    return pl.pallas_call(
        _compress_kernel,
        grid=(2, BATCH, NSA_GROUPS),
        in_specs=[
            pl.BlockSpec((SEQ, HEAD_DIM), lambda kv, b, g: (b, COL_KC + 2 * kv + g)),
            pl.BlockSpec((None, 1, CMP_BLOCK, HEAD_DIM), lambda kv, b, g: (layer, kv, 0, 0)),
            pl.BlockSpec((None, 1, CMP_BLOCK * HEAD_DIM, CMP_HIDDEN), lambda kv, b, g: (layer, kv, 0, 0)),
            pl.BlockSpec((None, 1, 1, CMP_HIDDEN), lambda kv, b, g: (layer, kv, 0, 0)),
            pl.BlockSpec((None, 1, CMP_HIDDEN, HEAD_DIM), lambda kv, b, g: (layer, kv, 0, 0)),
            pl.BlockSpec((None, 1, 1, HEAD_DIM), lambda kv, b, g: (layer, kv, 0, 0)),
        ],
        out_specs=pl.BlockSpec((1, 1, N_CMP_PAD, HEAD_DIM), lambda kv, b, g: (kv, b * NSA_GROUPS + g, 0, 0)),
        out_shape=jax.ShapeDtypeStruct((2, BATCH * NSA_GROUPS, N_CMP_PAD, HEAD_DIM), F32),
        scratch_shapes=[pltpu.VMEM((SEQ + CMP_BLOCK, HEAD_DIM), F32)],
        compiler_params=_params(("arbitrary", "arbitrary", "arbitrary")),
        name="nsa_compress",
    )(proj, pos, w1, b1, w2, b2)


def _rope(t, cs, sn):
    lane = lax.broadcasted_iota(jnp.int32, t.shape, 1)
    swapped = jnp.where(lane < ROT_HALF, pltpu.roll(t, HEAD_DIM - ROT_HALF, axis=1), pltpu.roll(t, ROT_HALF, axis=1))
    return t * cs + swapped * sn


def _dot_nt(a, b):
    return lax.dot_general(a, b, (((1,), (1,)), ((), ())), preferred_element_type=F32)


MASKED = -1e30
GROUP_TILES = 2


def _nsa_kernel(q_ref, kc_ref, vc_ref, ks_ref, vs_ref, kw_ref, vw_ref, gl_ref, cs_ref, sn_ref, wov_ref,
                o_ref, kaug, vaug, kwr, vwaug):
    scale = HEAD_DIM ** -0.5
    neg_inf = -jnp.inf
    row = lax.broadcasted_iota(jnp.int32, (SEQ, HEAD_DIM), 0)
    lane = lax.broadcasted_iota(jnp.int32, (SEQ, HEAD_DIM), 1)
    kaug[:, 0:HEAD_DIM] = _rope(ks_ref[...], cs_ref[...], sn_ref[...]).astype(BF16)
    kaug[:, HEAD_DIM:2 * HEAD_DIM] = jnp.where(lane == (row >> 6), 1.0, 0.0).astype(BF16)
    kwr[...] = _rope(kw_ref[...], cs_ref[...], sn_ref[...]).astype(BF16)
    ones = jnp.ones((SEQ, HEAD_DIM), BF16)
    vaug[:, 0:HEAD_DIM] = vs_ref[...].astype(BF16)
    vaug[:, HEAD_DIM:2 * HEAD_DIM] = ones
    vwaug[:, 0:HEAD_DIM] = vw_ref[...].astype(BF16)
    vwaug[:, HEAD_DIM:2 * HEAD_DIM] = ones
    kcb = kc_ref[0, 0].astype(BF16)
    vcb = vc_ref[0, 0].astype(BF16)

    def tile(i, extent):
        t0 = pl.multiple_of(i * Q_TILE, Q_TILE)
        q4 = q_ref[pl.ds(t0, Q_TILE), :] * scale
        cs = cs_ref[pl.ds(t0, Q_TILE), :]
        sn = sn_ref[pl.ds(t0, Q_TILE), :]
        gate = jax.nn.sigmoid(gl_ref[pl.ds(t0, Q_TILE), :])
        qh = [q4[:, h * HEAD_DIM:(h + 1) * HEAD_DIM] for h in range(NSA_HPG)]

        qs = jnp.concatenate(qh, axis=0).astype(BF16)
        sc = _dot_nt(qs, kcb)
        rows = lax.broadcasted_iota(jnp.int32, sc.shape, 0)
        ncol = lax.broadcasted_iota(jnp.int32, sc.shape, 1)
        tok = t0 + (rows & (Q_TILE - 1))
        valid = ncol * CMP_STRIDE + (CMP_BLOCK - 1) <= tok
        s = jnp.where(valid, sc, neg_inf)
        m = jnp.max(s, axis=-1, keepdims=True)
        m = jnp.where(m == neg_inf, 0.0, m)
        e = jnp.where(valid, jnp.exp(s - m), 0.0)
        p = e / jnp.maximum(jnp.sum(e, axis=-1, keepdims=True), 1e-30)
        o_cmp = jnp.dot(p.astype(BF16), vcb, preferred_element_type=F32)

        p4 = p[0:Q_TILE] + p[Q_TILE:2 * Q_TILE] + p[2 * Q_TILE:3 * Q_TILE] + p[3 * Q_TILE:4 * Q_TILE]
        imp = jnp.dot(p4, wov_ref[...], preferred_element_type=F32)
        jj = lax.broadcasted_iota(jnp.int32, imp.shape, 1)
        tt = t0 + lax.broadcasted_iota(jnp.int32, imp.shape, 0)
        back = (tt >> 6) - jj
        forced = (jj == 0) | (back.astype(jnp.uint32) < N_LOCAL_FORCED)
        score = jnp.where(forced, jnp.inf, jnp.where(jj * SLC_BLOCK <= tt, imp, neg_inf))
        score_t = score.T[0:N_SLC]
        jrow = lax.broadcasted_iota(jnp.int32, score_t.shape, 0)
        rank = jnp.zeros(score_t.shape, F32)
        for jp in range(N_SLC):
            other = score_t[jp:jp + 1]
            later = jnp.where(jrow > jp, 1.0, 0.0)
            rank = rank + jnp.where(other > score_t, 1.0, jnp.where(other == score_t, later, 0.0))
        bias_t = jnp.where(rank < SLC_TOPN, 0.0, MASKED)
        sel_bias = jnp.concatenate([bias_t, jnp.zeros((HEAD_DIM - N_SLC, Q_TILE), F32)], axis=0).T

        tail = GROUP_TILES * Q_TILE
        rows4 = NSA_HPG * Q_TILE
        kpos = (extent - tail) + lax.broadcasted_iota(jnp.int32, (rows4, tail), 1)
        ok_tail = kpos <= t0 + (lax.broadcasted_iota(jnp.int32, (rows4, tail), 0) & (Q_TILE - 1))

        qr = jnp.concatenate([_rope(qh[h], cs, sn) for h in range(NSA_HPG)], axis=0)
        qa = jnp.concatenate([qr, jnp.concatenate([sel_bias] * NSA_HPG, axis=0)], axis=1).astype(BF16)
        s1 = _dot_nt(qa, kaug[0:extent, :])
        s1_tail = jnp.where(ok_tail, s1[:, extent - tail:], MASKED)
        s1 = s1_tail if extent == tail else jnp.concatenate([s1[:, :extent - tail], s1_tail], axis=1)
        e1 = jnp.exp(s1 - jnp.max(s1, axis=-1, keepdims=True)).astype(BF16)
        r1 = jnp.dot(e1, vaug[0:extent, :], preferred_element_type=F32)
        o_slc = r1[:, :HEAD_DIM] / r1[:, HEAD_DIM:HEAD_DIM + 1]

        ws = pl.multiple_of(jnp.clip(t0 - WINDOW, 0, SEQ - WIN_KEYS), Q_TILE)
        wk = ws + lax.broadcasted_iota(jnp.int32, (rows4, WIN_KEYS), 1)
        wt = t0 + (lax.broadcasted_iota(jnp.int32, (rows4, WIN_KEYS), 0) & (Q_TILE - 1))
        ok_win = (wt - wk).astype(jnp.uint32) < WINDOW
        s2 = jnp.where(ok_win, _dot_nt(qr.astype(BF16), kwr[pl.ds(ws, WIN_KEYS), :]), MASKED)
        e2 = jnp.exp(s2 - jnp.max(s2, axis=-1, keepdims=True)).astype(BF16)
        r2 = jnp.dot(e2, vwaug[pl.ds(ws, WIN_KEYS), :], preferred_element_type=F32)
        o_win = r2[:, :HEAD_DIM] / r2[:, HEAD_DIM:HEAD_DIM + 1]

        outs = []
        for h in range(NSA_HPG):
            hr = slice(h * Q_TILE, (h + 1) * Q_TILE)
            outs.append(gate[:, 3 * h:3 * h + 1] * o_cmp[hr] + gate[:, 3 * h + 1:3 * h + 2] * o_slc[hr]
                        + gate[:, 3 * h + 2:3 * h + 3] * o_win[hr])
        o_ref[pl.ds(t0, Q_TILE), :] = jnp.concatenate(outs, axis=1).astype(BF16)

    for grp in range(SEQ // (GROUP_TILES * Q_TILE)):
        extent = (grp + 1) * GROUP_TILES * Q_TILE

        def body(ii, carry, grp=grp, extent=extent):
            tile(grp * GROUP_TILES + ii, extent)
            return carry

        lax.fori_loop(0, GROUP_TILES, body, 0)


def _cmp_to_slc_weights():
    cs = np.arange(N_CMP) * CMP_STRIDE
    ce = cs + CMP_BLOCK
    ss = np.arange(N_SLC) * SLC_BLOCK
    se = ss + SLC_BLOCK
    ov = np.clip(np.minimum(ce[:, None], se[None, :]) - np.maximum(cs[:, None], ss[None, :]), 0, None)
    w = np.zeros((N_CMP_PAD, HEAD_DIM), np.float32)
    w[:N_CMP, :N_SLC] = ov / CMP_BLOCK
    return w


def nsa_attention(proj, gates, cmp_kv, rope_cs, rope_sn):
    gw = NSA_HPG * HEAD_DIM
    col = lambda c: pl.BlockSpec((SEQ, HEAD_DIM), lambda b, g: (b, c + g))
    return pl.pallas_call(
        _nsa_kernel,
        grid=(BATCH, NSA_GROUPS),
        in_specs=[
            pl.BlockSpec((SEQ, gw), lambda b, g: (b, g)),
            pl.BlockSpec((1, 1, N_CMP_PAD, HEAD_DIM), lambda b, g: (0, b * NSA_GROUPS + g, 0, 0)),
            pl.BlockSpec((1, 1, N_CMP_PAD, HEAD_DIM), lambda b, g: (1, b * NSA_GROUPS + g, 0, 0)),
            col(COL_KS), col(COL_VS), col(COL_KW), col(COL_VW), col(0),
            pl.BlockSpec((SEQ, HEAD_DIM), lambda b, g: (b, 0)),
            pl.BlockSpec((SEQ, HEAD_DIM), lambda b, g: (b, 0)),
            pl.BlockSpec((N_CMP_PAD, HEAD_DIM), lambda b, g: (0, 0)),
        ],
        out_specs=pl.BlockSpec((SEQ, gw), lambda b, g: (b, g)),
        out_shape=jax.ShapeDtypeStruct((TOKENS, NSA_HEADS * HEAD_DIM), BF16),
        scratch_shapes=[pltpu.VMEM((SEQ, 2 * HEAD_DIM), BF16), pltpu.VMEM((SEQ, 2 * HEAD_DIM), BF16),
                        pltpu.VMEM((SEQ, HEAD_DIM), BF16), pltpu.VMEM((SEQ, 2 * HEAD_DIM), BF16)],
        compiler_params=_params(("parallel", "parallel")),
        name="nsa_attention",
    )(proj, cmp_kv, cmp_kv, proj, proj, proj, proj, gates, rope_cs, rope_sn, jnp.asarray(_cmp_to_slc_weights()))


HGRN_HEADS_PER_STEP = 4


def _hgrn_chunk(qc, fc, vc, gc, lb, log_lb, log_1m, gn, tri, state_t):
    C, SUB = HGRN_CHUNK, HGRN_SUB
    row8 = lax.broadcasted_iota(jnp.int32, (8, HEAD_DIM), 0)
    qh = qc * jax.nn.sigmoid(qc)
    log_sig = jnp.minimum(fc, 0.0) - jnp.log(1.0 + jnp.exp(-jnp.abs(fc)))
    b = log_1m + log_sig
    lf = jnp.maximum(log_lb, b) + jnp.log(1.0 + jnp.exp(-jnp.abs(log_lb - b)))
    kh = (1.0 - lb) * jax.nn.sigmoid(-fc)
    a = jnp.dot(tri, lf, preferred_element_type=F32)
    inter = _dot_nt(qh * jnp.exp(a), state_t)
    parts = []
    for blk in range(C // SUB):
        lo = blk * SUB
        ab, qb, kb, vb = a[lo:lo + SUB], qh[lo:lo + SUB], kh[lo:lo + SUB], vc[lo:lo + SUB]
        acc = inter[lo:lo + SUB]
        if blk > 0:
            a_ref_row = a[lo - 1:lo]
            qt = qb * jnp.exp(ab - a_ref_row)
            kt = kh[0:lo] * jnp.exp(a_ref_row - a[0:lo])
            acc = acc + jnp.dot(_dot_nt(qt, kt), vc[0:lo], preferred_element_type=F32)
        for r0 in range(0, SUB, 8):
            ar, qr, acc_r = ab[r0:r0 + 8], qb[r0:r0 + 8], acc[r0:r0 + 8]
            for s in range(r0 + 8):
                rel = ar - ab[s:s + 1]
                if s >= r0:
                    rel = jnp.where(row8 >= s - r0, rel, -jnp.inf)
                w = jnp.sum(qr * kb[s:s + 1] * jnp.exp(rel), axis=-1, keepdims=True)
                acc_r = acc_r + w * vb[s:s + 1]
            parts.append(acc_r)
    o = jnp.concatenate(parts, axis=0)
    a_end = a[C - 1:C]
    kt_end = kh * jnp.exp(a_end - a)
    state_t = state_t * jnp.exp(a_end) + lax.dot_general(vc, kt_end, (((0,), (0,)), ((), ())),
                                                          preferred_element_type=F32)
    o = o * lax.rsqrt(jnp.mean(o * o, axis=-1, keepdims=True) + EPS) * gn
    return (o * jax.nn.sigmoid(gc)).astype(BF16), state_t


def _hgrn_kernel(q_ref, f_ref, i_ref, g_ref, lb_ref, gn_ref, tri_ref, o_ref):
    C, HP = HGRN_CHUNK, HGRN_HEADS_PER_STEP
    lbs = [lb_ref[hp] for hp in range(HP)]
    log_lbs = [jnp.log(lb) for lb in lbs]
    log_1ms = [jnp.log(1.0 - lb) for lb in lbs]
    gns = [gn_ref[hp] for hp in range(HP)]

    def chunk(c, states):
        r0 = pl.multiple_of(c * C, C)
        qc = q_ref[pl.ds(r0, C), :]
        fc = f_ref[pl.ds(r0, C), :]
        vc = i_ref[pl.ds(r0, C), :]
        gc = g_ref[pl.ds(r0, C), :]
        outs, new_states = [], []
        for hp in range(HP):
            cols = slice(hp * HEAD_DIM, (hp + 1) * HEAD_DIM)
            o, st = _hgrn_chunk(qc[:, cols], fc[:, cols], vc[:, cols], gc[:, cols], lbs[hp], log_lbs[hp], log_1ms[hp],
                                gns[hp], tri_ref[...], states[hp])
            outs.append(o)
            new_states.append(st)
        o_ref[pl.ds(r0, C), :] = jnp.concatenate(outs, axis=1)
        return tuple(new_states)

    lax.fori_loop(0, SEQ // C, chunk, tuple(jnp.zeros((HEAD_DIM, HEAD_DIM), F32) for _ in range(HP)))


def hgrn(proj, lb, gain):
    hp = HGRN_HEADS_PER_STEP
    col = lambda c: pl.BlockSpec((SEQ, hp * HEAD_DIM), lambda b, h: (b, c // hp + h))
    vec = pl.BlockSpec((hp, 1, HEAD_DIM), lambda b, h: (h, 0, 0))
    tri = np.tril(np.ones((HGRN_CHUNK, HGRN_CHUNK), np.float32))
    return pl.pallas_call(
        _hgrn_kernel,
        grid=(BATCH, HGRN_HEADS // hp),
        in_specs=[col(COL_HQ), col(COL_HF), col(COL_HI), col(COL_HG), vec, vec,
                  pl.BlockSpec((HGRN_CHUNK, HGRN_CHUNK), lambda b, h: (0, 0))],
        out_specs=pl.BlockSpec((SEQ, hp * HEAD_DIM), lambda b, h: (b, h)),
        out_shape=jax.ShapeDtypeStruct((TOKENS, HGRN_HEADS * HEAD_DIM), BF16),
        compiler_params=_params(("parallel", "parallel")),
        name="hgrn2",
    )(proj, proj, proj, proj, lb.reshape(HGRN_HEADS, 1, HEAD_DIM), gain.reshape(HGRN_HEADS, 1, HEAD_DIM),
      jnp.asarray(tri))


def _out_proj_kernel(on_ref, oh_ref, wn_ref, wh_ref, x_ref, g_ref, gain_ref, sh_ref, sc_ref, x1_ref, ht_ref):
    mix = (jnp.dot(on_ref[...], wn_ref[...], preferred_element_type=F32)
           + jnp.dot(oh_ref[...], wh_ref[...], preferred_element_type=F32))
    x1 = x_ref[...] + g_ref[0] * mix
    x1_ref[...] = x1
    ht_ref[...] = _rms_mod(x1, gain_ref[...], sc_ref[0], sh_ref[0]).T.astype(BF16)


def out_proj(o_nsa, o_hgrn, w_out, x, gain, mod, layer):
    tm = 256
    half = D_MODEL // 2
    rows = pl.BlockSpec((tm, half), lambda i: (i, 0))
    full = pl.BlockSpec((tm, D_MODEL), lambda i: (i, 0))
    return pl.pallas_call(
        _out_proj_kernel,
        grid=(TOKENS // tm,),
        in_specs=[rows, rows,
                  pl.BlockSpec((None, half, D_MODEL), lambda i: (layer, 0, 0)),
                  pl.BlockSpec((None, half, D_MODEL), lambda i: (layer, 1, 0)),
                  full, _mod_spec(layer, 2, SEQ // tm),
                  pl.BlockSpec((1, D_MODEL), lambda i: (0, 0)),
                  _mod_spec(layer, 3, SEQ // tm), _mod_spec(layer, 4, SEQ // tm)],
        out_specs=[full, pl.BlockSpec((D_MODEL, tm), lambda i: (0, i))],
        out_shape=[jax.ShapeDtypeStruct((TOKENS, D_MODEL), F32), jax.ShapeDtypeStruct((D_MODEL, TOKENS), BF16)],
        compiler_params=_params(("parallel",)),
        name="out_proj",
    )(o_nsa, o_hgrn, w_out, w_out, x, mod, gain.reshape(1, D_MODEL), mod, mod)


NOT_TOP = 127.0


def _pack_pairs(x):
    return pltpu.bitcast(x, jnp.uint32)


def _unpack_pairs(x):
    return pltpu.bitcast(x, BF16)


def _top_values(x, k, want_rank=False):
    vals = []
    rank = jnp.full(x.shape, NOT_TOP, F32) if want_rank else None
    for i in range(k):
        m = jnp.max(x, axis=0, keepdims=True)
        vals.append(m)
        hit = x == m
        if want_rank:
            rank = jnp.where(hit, float(i), rank)
        x = jnp.where(hit, -jnp.inf, x)
    return vals, rank


def _dot_tn(a, b):
    return lax.dot_general(a, b, (((0,), (0,)), ((), ())), preferred_element_type=F32)


def _peer_score_kernel(ht_ref, wq_ref, keys_ref, eb_ref, rb_ref, ea_ref, cnt_ref):
    k = PEER_TOPK
    qt = _dot_tn(wq_ref[...], ht_ref[...])
    for h in range(PEER_HEADS):
        sa = jnp.dot(keys_ref[2 * h], qt[(2 * h) * PEER_KEYS:(2 * h + 1) * PEER_KEYS], preferred_element_type=F32)
        sb = jnp.dot(keys_ref[2 * h + 1], qt[(2 * h + 1) * PEER_KEYS:(2 * h + 2) * PEER_KEYS],
                     preferred_element_type=F32)
        top_a, _ = _top_values(sa, k)
        top_b, rank_b = _top_values(sb, k, want_rank=True)
        a16 = jnp.concatenate(top_a, axis=0)
        b16 = jnp.concatenate(top_b, axis=0)
        cand = jnp.concatenate([a16 + top_b[0]] + [a16[0:8] + top_b[j] for j in range(1, 8)]
                               + [top_a[0] + b16[8:16]], axis=0)
        best, _ = _top_values(cand, k)
        z = jnp.zeros_like(best[0])
        for v in best:
            z = z + jnp.exp(v - best[0])
        cnt = jnp.zeros_like(sa)
        for j in range(k):
            cnt = cnt + jnp.where(sa + top_b[j] >= best[k - 1], 1.0, 0.0)
        half = slice(h * PEER_KEYS // 2, (h + 1) * PEER_KEYS // 2)
        eb_ref[half, :] = _pack_pairs(jnp.exp(sb - top_b[0]).astype(BF16))
        rb_ref[half, :] = _pack_pairs(rank_b.astype(BF16))
        ea_ref[h] = jnp.exp(sa - top_a[0]) / z
        cnt_ref[h] = cnt


def peer_scores(ht, wq, keys, layer):
    tm = 256
    rows = PEER_HEADS * PEER_KEYS
    spec2 = pl.BlockSpec((rows // 2, tm), lambda i: (0, i))
    shape2 = jax.ShapeDtypeStruct((rows // 2, TOKENS), jnp.uint32)
    spec3 = pl.BlockSpec((PEER_HEADS, PEER_KEYS, tm), lambda i: (0, 0, i))
    shape3 = jax.ShapeDtypeStruct((PEER_HEADS, PEER_KEYS, TOKENS), F32)
    return pl.pallas_call(
        _peer_score_kernel,
        grid=(TOKENS // tm,),
        in_specs=[pl.BlockSpec((D_MODEL, tm), lambda i: (0, i)),
                  pl.BlockSpec((None, D_MODEL, 2 * rows), lambda i: (layer, 0, 0)),
                  pl.BlockSpec((None, PEER_HEADS * 2, PEER_KEYS, PEER_KEYS), lambda i: (layer, 0, 0, 0))],
        out_specs=[spec2, spec2, spec3, spec3],
        out_shape=[shape2, shape2, shape3, shape3],
        compiler_params=_params(("parallel",)),
        name="peer_scores",
    )(ht, wq, keys)


PEER_TM = 1024
PEER_TE = 1024
PEER_LANES = 128


def _peer_dense_kernel(xt_ref, eb_ref, rb_ref, ea_ref, cnt_ref, u_ref, v_ref, o_ref, gw_ref):
    @pl.when(pl.program_id(1) == 0)
    def _():
        o_ref[...] = jnp.zeros_like(o_ref)

    def rows_bf16(ref, h, ab, cols):
        tile16 = jnp.broadcast_to(ref[h, ab:ab + 1, cols], (16, PEER_LANES)).astype(BF16)
        return jnp.concatenate([tile16] * (PEER_KEYS // 16), axis=0)

    act = jnp.dot(u_ref[...], xt_ref[...], preferred_element_type=F32)
    for ab in range(PEER_TE // PEER_KEYS):
        rows = slice(ab * PEER_KEYS, (ab + 1) * PEER_KEYS)
        for lt in range(PEER_TM // PEER_LANES):
            cols = slice(lt * PEER_LANES, (lt + 1) * PEER_LANES)
            w = jnp.zeros((PEER_KEYS, PEER_LANES), BF16)
            for h in range(PEER_HEADS):
                hb = slice(h * PEER_KEYS // 2, (h + 1) * PEER_KEYS // 2)
                eb = _unpack_pairs(eb_ref[hb, cols])
                rb = _unpack_pairs(rb_ref[hb, cols])
                cnt = rows_bf16(cnt_ref, h, ab, cols)
                w = w + jnp.where(rb < cnt, eb * rows_bf16(ea_ref, h, ab, cols), 0.0)
            gw = jax.nn.gelu(act[rows, cols].astype(BF16)) * w
            gw_ref[rows, cols] = gw
    o_ref[...] += _dot_tn(v_ref[...], gw_ref[...])


def peer_dense(ht, eb, rb, ea, cnt, u, v, layer):
    ab = PEER_TE // PEER_KEYS
    rows = pl.BlockSpec((PEER_HEADS, ab, PEER_TM), lambda i, j: (0, j, i))
    half = pl.BlockSpec((PEER_HEADS * PEER_KEYS // 2, PEER_TM), lambda i, j: (0, i))
    return pl.pallas_call(
        _peer_dense_kernel,
        grid=(TOKENS // PEER_TM, PEER_EXPERTS // PEER_TE),
        in_specs=[pl.BlockSpec((D_MODEL, PEER_TM), lambda i, j: (0, i)),
                  half, half, rows, rows,
                  pl.BlockSpec((None, PEER_TE, D_MODEL), lambda i, j: (layer, j, 0)),
                  pl.BlockSpec((None, PEER_TE, D_MODEL), lambda i, j: (layer, j, 0))],
        out_specs=pl.BlockSpec((D_MODEL, PEER_TM), lambda i, j: (0, i)),
        out_shape=jax.ShapeDtypeStruct((D_MODEL, TOKENS), F32),
        scratch_shapes=[pltpu.VMEM((PEER_TE, PEER_TM), BF16)],
        compiler_params=_params(("parallel", "arbitrary")),
        name="peer_dense",
    )(ht, eb, rb, ea, cnt, u, v)


def _peer_resid_kernel(x_ref, ot_ref, g_ref, gain_ref, sh_ref, sc_ref, x2_ref, h_ref):
    x2 = x_ref[...] + g_ref[0] * ot_ref[...].T
    x2_ref[...] = x2
    h_ref[...] = _rms_mod(x2, gain_ref[...], sc_ref[0], sh_ref[0]).astype(BF16)


def _final_kernel(x_ref, ot_ref, g_ref, gain_ref, o_ref):
    x2 = x_ref[...] + g_ref[0] * ot_ref[...].T
    o_ref[...] = x2 * lax.rsqrt(jnp.mean(x2 * x2, axis=-1, keepdims=True) + EPS) * gain_ref[...]


def peer_residual(x1, out_t, mod, layer, gain_next):
    tm = 256
    full = pl.BlockSpec((tm, D_MODEL), lambda i: (i, 0))
    return pl.pallas_call(
        _peer_resid_kernel,
        grid=(TOKENS // tm,),
        in_specs=[full, pl.BlockSpec((D_MODEL, tm), lambda i: (0, i)), _mod_spec(layer, 5, SEQ // tm),
                  pl.BlockSpec((1, D_MODEL), lambda i: (0, 0)),
                  _mod_spec(layer + 1, 0, SEQ // tm), _mod_spec(layer + 1, 1, SEQ // tm)],
        out_specs=[full, full],
        out_shape=[jax.ShapeDtypeStruct((TOKENS, D_MODEL), F32), jax.ShapeDtypeStruct((TOKENS, D_MODEL), BF16)],
        compiler_params=_params(("parallel",)),
        name="peer_residual",
    )(x1, out_t, mod, gain_next.reshape(1, D_MODEL), mod, mod)


def final_residual_norm(x1, out_t, mod, layer, gain):
    tm = 256
    full = pl.BlockSpec((tm, D_MODEL), lambda i: (i, 0))
    return pl.pallas_call(
        _final_kernel,
        grid=(TOKENS // tm,),
        in_specs=[full, pl.BlockSpec((D_MODEL, tm), lambda i: (0, i)), _mod_spec(layer, 5, SEQ // tm),
                  pl.BlockSpec((1, D_MODEL), lambda i: (0, 0))],
        out_specs=full,
        out_shape=jax.ShapeDtypeStruct((TOKENS, D_MODEL), F32),
        compiler_params=_params(("parallel",)),
        name="final_norm",
    )(x1, out_t, mod, gain.reshape(1, D_MODEL))


def _rope_tables(positions):
    inv = ROPE_THETA ** (-jnp.arange(0, 2 * ROT_HALF, 2, dtype=F32) / (2 * ROT_HALF))
    ang = positions.astype(F32).reshape(TOKENS, 1) * inv
    cos, sin = jnp.cos(ang), jnp.sin(ang)
    rest = HEAD_DIM - 2 * ROT_HALF
    cs = jnp.concatenate([cos, cos, jnp.ones((TOKENS, rest), F32)], axis=1)
    sn = jnp.concatenate([-sin, sin, jnp.zeros((TOKENS, rest), F32)], axis=1)
    return cs, sn


def kernel(x, c, positions, norm_mix, norm_ffn, w_ada, b_ada, w_in, w_out, cmp_k_pos, cmp_k_w1, cmp_k_b1, cmp_k_w2, cmp_k_b2, cmp_v_pos, cmp_v_w1, cmp_v_b1, cmp_v_w2, cmp_v_b2, hgrn_norm, hgrn_lb, peer_wq, peer_keys, peer_u, peer_v, final_norm):
    xf = x.reshape(TOKENS, D_MODEL)
    rope_cs, rope_sn = _rope_tables(positions)
    c8 = jnp.concatenate([c, jnp.zeros((8 - BATCH, D_MODEL), F32)], axis=0)
    mod = ada_mod(c8, w_ada, b_ada).reshape(DEPTH * 8 * 6, 1, D_MODEL)
    lb_all = jnp.cumsum(jax.nn.softmax(hgrn_lb.astype(F32), axis=0), axis=0)
    lb_all = lb_all - lb_all[0:1]

    w_nsa, w_hgrn, w_gate = split_w_in(w_in)
    w_out_b = w_out.astype(BF16)
    cmp_pos = jnp.stack([cmp_k_pos, cmp_v_pos], axis=1)
    cmp_w1 = jnp.stack([cmp_k_w1, cmp_v_w1], axis=1).astype(BF16)
    cmp_b1 = jnp.stack([cmp_k_b1, cmp_v_b1], axis=1).reshape(DEPTH, 2, 1, CMP_HIDDEN)
    cmp_w2 = jnp.stack([cmp_k_w2, cmp_v_w2], axis=1).astype(BF16)
    cmp_b2 = jnp.stack([cmp_k_b2, cmp_v_b2], axis=1).reshape(DEPTH, 2, 1, HEAD_DIM)
    wq_b = peer_wq.astype(BF16)
    keys = peer_keys.reshape(DEPTH, PEER_HEADS * 2, PEER_KEYS, PEER_KEYS)
    u_b = peer_u.astype(BF16)
    v_b = peer_v.astype(BF16)

    h = adaln(xf, norm_mix[0], mod, 0)
    for l in range(DEPTH):
        p_nsa = in_proj(h, w_nsa, l)
        p_hgrn = in_proj(h, w_hgrn, l)
        p_gate = in_proj(h, w_gate, l)
        cmp_kv = compress(p_nsa, cmp_pos, cmp_w1, cmp_b1, cmp_w2, cmp_b2, l)
        o_nsa = nsa_attention(p_nsa, p_gate, cmp_kv, rope_cs, rope_sn)
        o_hgrn = hgrn(p_hgrn, lb_all[l], hgrn_norm[l])
        x1, ht = out_proj(o_nsa, o_hgrn, w_out_b, xf, norm_ffn[l], mod, l)
        eb, rb, ea, cnt = peer_scores(ht, wq_b, keys, l)
        out_t = peer_dense(ht, eb, rb, ea, cnt, u_b, v_b, l)
        if l + 1 < DEPTH:
            xf, h = peer_residual(x1, out_t, mod, l, norm_mix[l + 1])
        else:
            xf = final_residual_norm(x1, out_t, mod, l, final_norm)
    return xf.reshape(BATCH, SEQ, D_MODEL)
```

```python
import numpy as np
import jax
import jax.numpy as jnp
from jax import lax
from jax.experimental import pallas as pl
from jax.experimental.pallas import tpu as pltpu

F32 = jnp.float32
BF16 = jnp.bfloat16

D_MODEL = 2048
BATCH = 4
SEQ = 2048
TOKENS = BATCH * SEQ
DEPTH = 2
EPS = 1e-6

HEAD_DIM = 128
NSA_HEADS = 8
NSA_GROUPS = 2
NSA_HPG = NSA_HEADS // NSA_GROUPS
CMP_BLOCK = 32
CMP_STRIDE = 16
CMP_HIDDEN = 256
N_CMP = (SEQ - CMP_BLOCK) // CMP_STRIDE + 1
N_CMP_PAD = 128
SLC_BLOCK = 64
N_SLC = SEQ // SLC_BLOCK
SLC_TOPN = 16
N_LOCAL_FORCED = 2
WINDOW = 512
ROT_HALF = 16
ROPE_THETA = 500000.0
Q_TILE = 128
WIN_KEYS = WINDOW + Q_TILE

HGRN_HEADS = 8
HGRN_CHUNK = 64
HGRN_SUB = 16

PEER_HEADS = 8
PEER_KEYS = 128
PEER_EXPERTS = PEER_KEYS * PEER_KEYS
PEER_TOPK = 16

NSA_COLS = 2560
COL_Q, COL_KC, COL_VC, COL_KS, COL_VS, COL_KW, COL_VW = 0, 8, 10, 12, 14, 16, 18
COL_HQ, COL_HF, COL_HI, COL_HG = 0, 8, 16, 24

VMEM_LIMIT = 60 * 1024 * 1024


def _params(sem):
    return pltpu.CompilerParams(dimension_semantics=sem, vmem_limit_bytes=VMEM_LIMIT)


def _ada_kernel(c_ref, w_ref, b_ref, o_ref):
    c = c_ref[...]
    o_ref[0] = jnp.dot(c * jax.nn.sigmoid(c), w_ref[0], preferred_element_type=F32) + b_ref[0]


def ada_mod(c8, w_ada, b_ada):
    tn = 1024
    n = 6 * D_MODEL
    return pl.pallas_call(
        _ada_kernel,
        grid=(DEPTH, n // tn),
        in_specs=[
            pl.BlockSpec((8, D_MODEL), lambda l, j: (0, 0)),
            pl.BlockSpec((1, D_MODEL, tn), lambda l, j: (l, 0, j)),
            pl.BlockSpec((1, 1, tn), lambda l, j: (l, 0, j)),
        ],
        out_specs=pl.BlockSpec((1, 8, tn), lambda l, j: (l, 0, j)),
        out_shape=jax.ShapeDtypeStruct((DEPTH, 8, n), F32),
        compiler_params=_params(("parallel", "parallel")),
        name="ada_mod",
    )(c8, w_ada, b_ada.reshape(DEPTH, 1, n))


def _mod_spec(layer, k, rows_per_batch):
    return pl.BlockSpec((1, 1, D_MODEL), lambda i: ((layer * 8 + i // rows_per_batch) * 6 + k, 0, 0))


def _rms_mod(x, gain, scale, shift):
    y = x * lax.rsqrt(jnp.mean(x * x, axis=-1, keepdims=True) + EPS) * gain
    return y * (1.0 + scale) + shift


def _adaln_kernel(x_ref, gain_ref, sh_ref, sc_ref, h_ref):
    h_ref[...] = _rms_mod(x_ref[...], gain_ref[...], sc_ref[0], sh_ref[0]).astype(BF16)


def adaln(x, gain, mod, layer):
    tm = 256
    return pl.pallas_call(
        _adaln_kernel,
        grid=(TOKENS // tm,),
        in_specs=[
            pl.BlockSpec((tm, D_MODEL), lambda i: (i, 0)),
            pl.BlockSpec((1, D_MODEL), lambda i: (0, 0)),
            _mod_spec(layer, 0, SEQ // tm),
            _mod_spec(layer, 1, SEQ // tm),
        ],
        out_specs=pl.BlockSpec((tm, D_MODEL), lambda i: (i, 0)),
        out_shape=jax.ShapeDtypeStruct((TOKENS, D_MODEL), BF16),
        compiler_params=_params(("parallel",)),
        name="adaln",
    )(x, gain.reshape(1, D_MODEL), mod, mod)


def _matmul_nt_kernel(a_ref, bt_ref, o_ref):
    o_ref[...] = lax.dot_general(a_ref[...], bt_ref[...], (((1,), (1,)), ((), ())), preferred_element_type=F32)


def in_proj(h, wt, layer):
    n = wt.shape[1]
    tm = 1024
    tn = next(t for t in (1280, 1024, 256) if n % t == 0)
    return pl.pallas_call(
        _matmul_nt_kernel,
        grid=(TOKENS // tm, n // tn),
        in_specs=[
            pl.BlockSpec((tm, D_MODEL), lambda i, j: (i, 0)),
            pl.BlockSpec((None, tn, D_MODEL), lambda i, j: (layer, j, 0)),
        ],
        out_specs=pl.BlockSpec((tm, tn), lambda i, j: (i, j)),
        out_shape=jax.ShapeDtypeStruct((TOKENS, n), F32),
        compiler_params=_params(("parallel", "parallel")),
        name="in_proj",
    )(h, wt)


def split_w_in(w):
    wt = jnp.swapaxes(w, 1, 2)
    gl = wt[:, NSA_COLS:NSA_COLS + 24]
    z = jnp.zeros((w.shape[0], 116, D_MODEL), w.dtype)
    gates = jnp.concatenate([gl[:, :12], z, gl[:, 12:], z], axis=1)
    return wt[:, :NSA_COLS].astype(BF16), wt[:, NSA_COLS + 24:].astype(BF16), gates.astype(BF16)


def _compress_kernel(t_ref, pos_ref, w1_ref, b1_ref, w2_ref, b2_ref, o_ref, pad_ref):
    pad_ref[0:SEQ, :] = t_ref[...]
    pad_ref[SEQ:SEQ + CMP_BLOCK, :] = jnp.zeros((CMP_BLOCK, HEAD_DIM), F32)
    acc = jnp.zeros((N_CMP_PAD, CMP_HIDDEN), F32)
    for l in range(CMP_BLOCK):
        tl = pad_ref[pl.ds(l, N_CMP_PAD, stride=CMP_STRIDE), :] + pos_ref[0, l:l + 1, :]
        acc = acc + jnp.dot(tl.astype(BF16), w1_ref[0, l * HEAD_DIM:(l + 1) * HEAD_DIM, :],
                            preferred_element_type=F32)
    hid = jax.nn.gelu(acc + b1_ref[0])
    o_ref[0, 0] = jnp.dot(hid.astype(BF16), w2_ref[0], preferred_element_type=F32) + b2_ref[0]


def compress(proj, pos, w1, b1, w2, b2, layer):
    return pl.pallas_call(
        _compress_kernel,
        grid=(2, BATCH, NSA_GROUPS),
        in_specs=[
            pl.BlockSpec((SEQ, HEAD_DIM), lambda kv, b, g: (b, COL_KC + 2 * kv + g)),
            pl.BlockSpec((None, 1, CMP_BLOCK, HEAD_DIM), lambda kv, b, g: (layer, kv, 0, 0)),
            pl.BlockSpec((None, 1, CMP_BLOCK * HEAD_DIM, CMP_HIDDEN), lambda kv, b, g: (layer, kv, 0, 0)),
            pl.BlockSpec((None, 1, 1, CMP_HIDDEN), lambda kv, b, g: (layer, kv, 0, 0)),
            pl.BlockSpec((None, 1, CMP_HIDDEN, HEAD_DIM), lambda kv, b, g: (layer, kv, 0, 0)),
            pl.BlockSpec((None, 1, 1, HEAD_DIM), lambda kv, b, g: (layer, kv, 0, 0)),
        ],
        out_specs=pl.BlockSpec((1, 1, N_CMP_PAD, HEAD_DIM), lambda kv, b, g: (kv, b * NSA_GROUPS + g, 0, 0)),
        out_shape=jax.ShapeDtypeStruct((2, BATCH * NSA_GROUPS, N_CMP_PAD, HEAD_DIM), F32),
        scratch_shapes=[pltpu.VMEM((SEQ + CMP_BLOCK, HEAD_DIM), F32)],
        compiler_params=_params(("arbitrary", "arbitrary", "arbitrary")),
        name="nsa_compress",
    )(proj, pos, w1, b1, w2, b2)


def _rope(t, cs, sn):
    lane = lax.broadcasted_iota(jnp.int32, t.shape, 1)
    swapped = jnp.where(lane < ROT_HALF, pltpu.roll(t, HEAD_DIM - ROT_HALF, axis=1), pltpu.roll(t, ROT_HALF, axis=1))
    return t * cs + swapped * sn


def _dot_nt(a, b):
    return lax.dot_general(a, b, (((1,), (1,)), ((), ())), preferred_element_type=F32)


MASKED = -1e30
GROUP_TILES = 2


def _nsa_kernel(q_ref, kc_ref, vc_ref, ks_ref, vs_ref, kw_ref, vw_ref, gl_ref, cs_ref, sn_ref, wov_ref,
                o_ref, kaug, vaug, kwr, vwaug):
    scale = HEAD_DIM ** -0.5
    neg_inf = -jnp.inf
    row = lax.broadcasted_iota(jnp.int32, (SEQ, HEAD_DIM), 0)
    lane = lax.broadcasted_iota(jnp.int32, (SEQ, HEAD_DIM), 1)
    kaug[:, 0:HEAD_DIM] = _rope(ks_ref[...], cs_ref[...], sn_ref[...]).astype(BF16)
    kaug[:, HEAD_DIM:2 * HEAD_DIM] = jnp.where(lane == (row >> 6), 1.0, 0.0).astype(BF16)
    kwr[...] = _rope(kw_ref[...], cs_ref[...], sn_ref[...]).astype(BF16)
    ones = jnp.ones((SEQ, HEAD_DIM), BF16)
    vaug[:, 0:HEAD_DIM] = vs_ref[...].astype(BF16)
    vaug[:, HEAD_DIM:2 * HEAD_DIM] = ones
    vwaug[:, 0:HEAD_DIM] = vw_ref[...].astype(BF16)
    vwaug[:, HEAD_DIM:2 * HEAD_DIM] = ones
    kcb = kc_ref[0, 0].astype(BF16)
    vcb = vc_ref[0, 0].astype(BF16)

    def tile(i, extent):
        t0 = pl.multiple_of(i * Q_TILE, Q_TILE)
        q4 = q_ref[pl.ds(t0, Q_TILE), :] * scale
        cs = cs_ref[pl.ds(t0, Q_TILE), :]
        sn = sn_ref[pl.ds(t0, Q_TILE), :]
        gate = jax.nn.sigmoid(gl_ref[pl.ds(t0, Q_TILE), :])
        qh = [q4[:, h * HEAD_DIM:(h + 1) * HEAD_DIM] for h in range(NSA_HPG)]

        qs = jnp.concatenate(qh, axis=0).astype(BF16)
        sc = _dot_nt(qs, kcb)
        rows = lax.broadcasted_iota(jnp.int32, sc.shape, 0)
        ncol = lax.broadcasted_iota(jnp.int32, sc.shape, 1)
        tok = t0 + (rows & (Q_TILE - 1))
        valid = ncol * CMP_STRIDE + (CMP_BLOCK - 1) <= tok
        s = jnp.where(valid, sc, neg_inf)
        m = jnp.max(s, axis=-1, keepdims=True)
        m = jnp.where(m == neg_inf, 0.0, m)
        e = jnp.where(valid, jnp.exp(s - m), 0.0)
        p = e / jnp.maximum(jnp.sum(e, axis=-1, keepdims=True), 1e-30)
        o_cmp = jnp.dot(p.astype(BF16), vcb, preferred_element_type=F32)

        p4 = p[0:Q_TILE] + p[Q_TILE:2 * Q_TILE] + p[2 * Q_TILE:3 * Q_TILE] + p[3 * Q_TILE:4 * Q_TILE]
        imp = jnp.dot(p4, wov_ref[...], preferred_element_type=F32)
        jj = lax.broadcasted_iota(jnp.int32, imp.shape, 1)
        tt = t0 + lax.broadcasted_iota(jnp.int32, imp.shape, 0)
        back = (tt >> 6) - jj
        forced = (jj == 0) | (back.astype(jnp.uint32) < N_LOCAL_FORCED)
        score = jnp.where(forced, jnp.inf, jnp.where(jj * SLC_BLOCK <= tt, imp, neg_inf))
        score_t = score.T[0:N_SLC]
        jrow = lax.broadcasted_iota(jnp.int32, score_t.shape, 0)
        rank = jnp.zeros(score_t.shape, F32)
        for jp in range(N_SLC):
            other = score_t[jp:jp + 1]
            later = jnp.where(jrow > jp, 1.0, 0.0)
            rank = rank + jnp.where(other > score_t, 1.0, jnp.where(other == score_t, later, 0.0))
        bias_t = jnp.where(rank < SLC_TOPN, 0.0, MASKED)
        sel_bias = jnp.concatenate([bias_t, jnp.zeros((HEAD_DIM - N_SLC, Q_TILE), F32)], axis=0).T

        tail = GROUP_TILES * Q_TILE
        rows4 = NSA_HPG * Q_TILE
        kpos = (extent - tail) + lax.broadcasted_iota(jnp.int32, (rows4, tail), 1)
        ok_tail = kpos <= t0 + (lax.broadcasted_iota(jnp.int32, (rows4, tail), 0) & (Q_TILE - 1))

        qr = jnp.concatenate([_rope(qh[h], cs, sn) for h in range(NSA_HPG)], axis=0)
        qa = jnp.concatenate([qr, jnp.concatenate([sel_bias] * NSA_HPG, axis=0)], axis=1).astype(BF16)
        s1 = _dot_nt(qa, kaug[0:extent, :])
        s1_tail = jnp.where(ok_tail, s1[:, extent - tail:], MASKED)
        s1 = s1_tail if extent == tail else jnp.concatenate([s1[:, :extent - tail], s1_tail], axis=1)
        e1 = jnp.exp(s1 - jnp.max(s1, axis=-1, keepdims=True)).astype(BF16)
        r1 = jnp.dot(e1, vaug[0:extent, :], preferred_element_type=F32)
        o_slc = r1[:, :HEAD_DIM] / r1[:, HEAD_DIM:HEAD_DIM + 1]

        ws = pl.multiple_of(jnp.clip(t0 - WINDOW, 0, SEQ - WIN_KEYS), Q_TILE)
        wk = ws + lax.broadcasted_iota(jnp.int32, (rows4, WIN_KEYS), 1)
        wt = t0 + (lax.broadcasted_iota(jnp.int32, (rows4, WIN_KEYS), 0) & (Q_TILE - 1))
        ok_win = (wt - wk).astype(jnp.uint32) < WINDOW
        s2 = jnp.where(ok_win, _dot_nt(qr.astype(BF16), kwr[pl.ds(ws, WIN_KEYS), :]), MASKED)
        e2 = jnp.exp(s2 - jnp.max(s2, axis=-1, keepdims=True)).astype(BF16)
        r2 = jnp.dot(e2, vwaug[pl.ds(ws, WIN_KEYS), :], preferred_element_type=F32)
        o_win = r2[:, :HEAD_DIM] / r2[:, HEAD_DIM:HEAD_DIM + 1]

        outs = []
        for h in range(NSA_HPG):
            hr = slice(h * Q_TILE, (h + 1) * Q_TILE)
            outs.append(gate[:, 3 * h:3 * h + 1] * o_cmp[hr] + gate[:, 3 * h + 1:3 * h + 2] * o_slc[hr]
                        + gate[:, 3 * h + 2:3 * h + 3] * o_win[hr])
        o_ref[pl.ds(t0, Q_TILE), :] = jnp.concatenate(outs, axis=1).astype(BF16)

    for grp in range(SEQ // (GROUP_TILES * Q_TILE)):
        extent = (grp + 1) * GROUP_TILES * Q_TILE

        def body(ii, carry, grp=grp, extent=extent):
            tile(grp * GROUP_TILES + ii, extent)
            return carry

        lax.fori_loop(0, GROUP_TILES, body, 0)


def _cmp_to_slc_weights():
    cs = np.arange(N_CMP) * CMP_STRIDE
    ce = cs + CMP_BLOCK
    ss = np.arange(N_SLC) * SLC_BLOCK
    se = ss + SLC_BLOCK
    ov = np.clip(np.minimum(ce[:, None], se[None, :]) - np.maximum(cs[:, None], ss[None, :]), 0, None)
    w = np.zeros((N_CMP_PAD, HEAD_DIM), np.float32)
    w[:N_CMP, :N_SLC] = ov / CMP_BLOCK
    return w


def nsa_attention(proj, gates, cmp_kv, rope_cs, rope_sn):
    gw = NSA_HPG * HEAD_DIM
    col = lambda c: pl.BlockSpec((SEQ, HEAD_DIM), lambda b, g: (b, c + g))
    return pl.pallas_call(
        _nsa_kernel,
        grid=(BATCH, NSA_GROUPS),
        in_specs=[
            pl.BlockSpec((SEQ, gw), lambda b, g: (b, g)),
            pl.BlockSpec((1, 1, N_CMP_PAD, HEAD_DIM), lambda b, g: (0, b * NSA_GROUPS + g, 0, 0)),
            pl.BlockSpec((1, 1, N_CMP_PAD, HEAD_DIM), lambda b, g: (1, b * NSA_GROUPS + g, 0, 0)),
            col(COL_KS), col(COL_VS), col(COL_KW), col(COL_VW), col(0),
            pl.BlockSpec((SEQ, HEAD_DIM), lambda b, g: (b, 0)),
            pl.BlockSpec((SEQ, HEAD_DIM), lambda b, g: (b, 0)),
            pl.BlockSpec((N_CMP_PAD, HEAD_DIM), lambda b, g: (0, 0)),
        ],
        out_specs=pl.BlockSpec((SEQ, gw), lambda b, g: (b, g)),
        out_shape=jax.ShapeDtypeStruct((TOKENS, NSA_HEADS * HEAD_DIM), BF16),
        scratch_shapes=[pltpu.VMEM((SEQ, 2 * HEAD_DIM), BF16), pltpu.VMEM((SEQ, 2 * HEAD_DIM), BF16),
                        pltpu.VMEM((SEQ, HEAD_DIM), BF16), pltpu.VMEM((SEQ, 2 * HEAD_DIM), BF16)],
        compiler_params=_params(("parallel", "parallel")),
        name="nsa_attention",
    )(proj, cmp_kv, cmp_kv, proj, proj, proj, proj, gates, rope_cs, rope_sn, jnp.asarray(_cmp_to_slc_weights()))


HGRN_HEADS_PER_STEP = 4


def _hgrn_chunk(qc, fc, vc, gc, lb, log_lb, log_1m, gn, tri, state_t):
    C, SUB = HGRN_CHUNK, HGRN_SUB
    row8 = lax.broadcasted_iota(jnp.int32, (8, HEAD_DIM), 0)
    qh = qc * jax.nn.sigmoid(qc)
    log_sig = jnp.minimum(fc, 0.0) - jnp.log(1.0 + jnp.exp(-jnp.abs(fc)))
    b = log_1m + log_sig
    lf = jnp.maximum(log_lb, b) + jnp.log(1.0 + jnp.exp(-jnp.abs(log_lb - b)))
    kh = (1.0 - lb) * jax.nn.sigmoid(-fc)
    a = jnp.dot(tri, lf, preferred_element_type=F32)
    inter = _dot_nt(qh * jnp.exp(a), state_t)
    parts = []
    for blk in range(C // SUB):
        lo = blk * SUB
        ab, qb, kb, vb = a[lo:lo + SUB], qh[lo:lo + SUB], kh[lo:lo + SUB], vc[lo:lo + SUB]
        acc = inter[lo:lo + SUB]
        if blk > 0:
            a_ref_row = a[lo - 1:lo]
            qt = qb * jnp.exp(ab - a_ref_row)
            kt = kh[0:lo] * jnp.exp(a_ref_row - a[0:lo])
            acc = acc + jnp.dot(_dot_nt(qt, kt), vc[0:lo], preferred_element_type=F32)
        for r0 in range(0, SUB, 8):
            ar, qr, acc_r = ab[r0:r0 + 8], qb[r0:r0 + 8], acc[r0:r0 + 8]
            for s in range(r0 + 8):
                rel = ar - ab[s:s + 1]
                if s >= r0:
                    rel = jnp.where(row8 >= s - r0, rel, -jnp.inf)
                w = jnp.sum(qr * kb[s:s + 1] * jnp.exp(rel), axis=-1, keepdims=True)
                acc_r = acc_r + w * vb[s:s + 1]
            parts.append(acc_r)
    o = jnp.concatenate(parts, axis=0)
    a_end = a[C - 1:C]
    kt_end = kh * jnp.exp(a_end - a)
    state_t = state_t * jnp.exp(a_end) + lax.dot_general(vc, kt_end, (((0,), (0,)), ((), ())),
                                                          preferred_element_type=F32)
    o = o * lax.rsqrt(jnp.mean(o * o, axis=-1, keepdims=True) + EPS) * gn
    return (o * jax.nn.sigmoid(gc)).astype(BF16), state_t


def _hgrn_kernel(q_ref, f_ref, i_ref, g_ref, lb_ref, gn_ref, tri_ref, o_ref):
    C, HP = HGRN_CHUNK, HGRN_HEADS_PER_STEP
    lbs = [lb_ref[hp] for hp in range(HP)]
    log_lbs = [jnp.log(lb) for lb in lbs]
    log_1ms = [jnp.log(1.0 - lb) for lb in lbs]
    gns = [gn_ref[hp] for hp in range(HP)]

    def chunk(c, states):
        r0 = pl.multiple_of(c * C, C)
        qc = q_ref[pl.ds(r0, C), :]
        fc = f_ref[pl.ds(r0, C), :]
        vc = i_ref[pl.ds(r0, C), :]
        gc = g_ref[pl.ds(r0, C), :]
        outs, new_states = [], []
        for hp in range(HP):
            cols = slice(hp * HEAD_DIM, (hp + 1) * HEAD_DIM)
            o, st = _hgrn_chunk(qc[:, cols], fc[:, cols], vc[:, cols], gc[:, cols], lbs[hp], log_lbs[hp], log_1ms[hp],
                                gns[hp], tri_ref[...], states[hp])
            outs.append(o)
            new_states.append(st)
        o_ref[pl.ds(r0, C), :] = jnp.concatenate(outs, axis=1)
        return tuple(new_states)

    lax.fori_loop(0, SEQ // C, chunk, tuple(jnp.zeros((HEAD_DIM, HEAD_DIM), F32) for _ in range(HP)), unroll=4)


def hgrn(proj, lb, gain):
    hp = HGRN_HEADS_PER_STEP
    col = lambda c: pl.BlockSpec((SEQ, hp * HEAD_DIM), lambda b, h: (b, c // hp + h))
    vec = pl.BlockSpec((hp, 1, HEAD_DIM), lambda b, h: (h, 0, 0))
    tri = np.tril(np.ones((HGRN_CHUNK, HGRN_CHUNK), np.float32))
    return pl.pallas_call(
        _hgrn_kernel,
        grid=(BATCH, HGRN_HEADS // hp),
        in_specs=[col(COL_HQ), col(COL_HF), col(COL_HI), col(COL_HG), vec, vec,
                  pl.BlockSpec((HGRN_CHUNK, HGRN_CHUNK), lambda b, h: (0, 0))],
        out_specs=pl.BlockSpec((SEQ, hp * HEAD_DIM), lambda b, h: (b, h)),
        out_shape=jax.ShapeDtypeStruct((TOKENS, HGRN_HEADS * HEAD_DIM), BF16),
        compiler_params=_params(("parallel", "parallel")),
        name="hgrn2",
    )(proj, proj, proj, proj, lb.reshape(HGRN_HEADS, 1, HEAD_DIM), gain.reshape(HGRN_HEADS, 1, HEAD_DIM),
      jnp.asarray(tri))


def _out_proj_kernel(on_ref, oh_ref, wn_ref, wh_ref, x_ref, g_ref, gain_ref, sh_ref, sc_ref, x1_ref, ht_ref):
    mix = (jnp.dot(on_ref[...], wn_ref[...], preferred_element_type=F32)
           + jnp.dot(oh_ref[...], wh_ref[...], preferred_element_type=F32))
    x1 = x_ref[...] + g_ref[0] * mix
    x1_ref[...] = x1
    ht_ref[...] = _rms_mod(x1, gain_ref[...], sc_ref[0], sh_ref[0]).T.astype(BF16)


def out_proj(o_nsa, o_hgrn, w_out, x, gain, mod, layer):
    tm = 256
    half = D_MODEL // 2
    rows = pl.BlockSpec((tm, half), lambda i: (i, 0))
    full = pl.BlockSpec((tm, D_MODEL), lambda i: (i, 0))
    return pl.pallas_call(
        _out_proj_kernel,
        grid=(TOKENS // tm,),
        in_specs=[rows, rows,
                  pl.BlockSpec((None, half, D_MODEL), lambda i: (layer, 0, 0)),
                  pl.BlockSpec((None, half, D_MODEL), lambda i: (layer, 1, 0)),
                  full, _mod_spec(layer, 2, SEQ // tm),
                  pl.BlockSpec((1, D_MODEL), lambda i: (0, 0)),
                  _mod_spec(layer, 3, SEQ // tm), _mod_spec(layer, 4, SEQ // tm)],
        out_specs=[full, pl.BlockSpec((D_MODEL, tm), lambda i: (0, i))],
        out_shape=[jax.ShapeDtypeStruct((TOKENS, D_MODEL), F32), jax.ShapeDtypeStruct((D_MODEL, TOKENS), BF16)],
        compiler_params=_params(("parallel",)),
        name="out_proj",
    )(o_nsa, o_hgrn, w_out, w_out, x, mod, gain.reshape(1, D_MODEL), mod, mod)


NOT_TOP = 127.0


def _top_values(x, k, want_rank=False):
    vals = []
    rank = jnp.full(x.shape, NOT_TOP, F32) if want_rank else None
    for i in range(k):
        m = jnp.max(x, axis=0, keepdims=True)
        vals.append(m)
        hit = x == m
        if want_rank:
            rank = jnp.where(hit, float(i), rank)
        x = jnp.where(hit, -jnp.inf, x)
    return vals, rank


def _dot_tn(a, b):
    return lax.dot_general(a, b, (((0,), (0,)), ((), ())), preferred_element_type=F32)


def _peer_score_kernel(ht_ref, wq_ref, keys_ref, eb_ref, rb_ref, ea_ref, cnt_ref):
    k = PEER_TOPK
    qt = _dot_tn(wq_ref[...], ht_ref[...])
    for h in range(PEER_HEADS):
        sa = jnp.dot(keys_ref[2 * h], qt[(2 * h) * PEER_KEYS:(2 * h + 1) * PEER_KEYS], preferred_element_type=F32)
        sb = jnp.dot(keys_ref[2 * h + 1], qt[(2 * h + 1) * PEER_KEYS:(2 * h + 2) * PEER_KEYS],
                     preferred_element_type=F32)
        top_a, _ = _top_values(sa, k)
        top_b, rank_b = _top_values(sb, k, want_rank=True)
        a16 = jnp.concatenate(top_a, axis=0)
        b16 = jnp.concatenate(top_b, axis=0)
        cand = jnp.concatenate([a16 + top_b[0]] + [a16[0:8] + top_b[j] for j in range(1, 8)]
                               + [top_a[0] + b16[8:16]], axis=0)
        best, _ = _top_values(cand, k)
        z = jnp.zeros_like(best[0])
        for v in best:
            z = z + jnp.exp(v - best[0])
        cnt = jnp.zeros_like(sa)
        for j in range(k):
            cnt = cnt + jnp.where(sa + top_b[j] >= best[k - 1], 1.0, 0.0)
        eb_ref[h * PEER_KEYS:(h + 1) * PEER_KEYS, :] = jnp.exp(sb - top_b[0]).astype(BF16)
        rb_ref[h * PEER_KEYS:(h + 1) * PEER_KEYS, :] = rank_b.astype(BF16)
        ea_ref[h] = jnp.exp(sa - top_a[0]) / z
        cnt_ref[h] = cnt


def peer_scores(ht, wq, keys, layer):
    tm = 256
    rows = PEER_HEADS * PEER_KEYS
    spec2 = pl.BlockSpec((rows, tm), lambda i: (0, i))
    shape2 = jax.ShapeDtypeStruct((rows, TOKENS), BF16)
    spec3 = pl.BlockSpec((PEER_HEADS, PEER_KEYS, tm), lambda i: (0, 0, i))
    shape3 = jax.ShapeDtypeStruct((PEER_HEADS, PEER_KEYS, TOKENS), F32)
    return pl.pallas_call(
        _peer_score_kernel,
        grid=(TOKENS // tm,),
        in_specs=[pl.BlockSpec((D_MODEL, tm), lambda i: (0, i)),
                  pl.BlockSpec((None, D_MODEL, 2 * rows), lambda i: (layer, 0, 0)),
                  pl.BlockSpec((None, PEER_HEADS * 2, PEER_KEYS, PEER_KEYS), lambda i: (layer, 0, 0, 0))],
        out_specs=[spec2, spec2, spec3, spec3],
        out_shape=[shape2, shape2, shape3, shape3],
        compiler_params=_params(("parallel",)),
        name="peer_scores",
    )(ht, wq, keys)


PEER_TM = 1024
PEER_TE = 1024
PEER_LANES = 128


def _gelu_to_bf16(x):
    c = float(np.sqrt(2.0 / np.pi))
    t = jnp.tanh((x * (c + (c * 0.044715) * (x * x))).astype(BF16))
    return (0.5 * x.astype(BF16)) * (1.0 + t)


def _peer_dense_kernel(xt_ref, eb_in_ref, rb_in_ref, ea_ref, cnt_ref, u_ref, v_ref, o_ref, gw_ref, eb_ref, rb_ref):
    @pl.when(pl.program_id(1) == 0)
    def _():
        o_ref[...] = jnp.zeros_like(o_ref)
        eb_ref[...] = eb_in_ref[...]
        rb_ref[...] = rb_in_ref[...]

    def rows_bf16(ref, h, ab, cols):
        tile16 = jnp.broadcast_to(ref[h, ab:ab + 1, cols], (16, PEER_LANES)).astype(BF16)
        return jnp.concatenate([tile16] * (PEER_KEYS // 16), axis=0)

    act = jnp.dot(u_ref[...], xt_ref[...], preferred_element_type=F32)
    for ab in range(PEER_TE // PEER_KEYS):
        rows = slice(ab * PEER_KEYS, (ab + 1) * PEER_KEYS)
        for lt in range(PEER_TM // PEER_LANES):
            cols = slice(lt * PEER_LANES, (lt + 1) * PEER_LANES)
            w = jnp.zeros((PEER_KEYS, PEER_LANES), BF16)
            for h in range(PEER_HEADS):
                hb = slice(h * PEER_KEYS, (h + 1) * PEER_KEYS)
                cnt = rows_bf16(cnt_ref, h, ab, cols)
                w = w + jnp.where(rb_ref[hb, cols] < cnt, eb_ref[hb, cols] * rows_bf16(ea_ref, h, ab, cols), 0.0)
            gw = _gelu_to_bf16(act[rows, cols]) * w
            gw_ref[rows, cols] = gw
    o_ref[...] += _dot_tn(v_ref[...], gw_ref[...])


def peer_dense(ht, eb, rb, ea, cnt, u, v, layer):
    ab = PEER_TE // PEER_KEYS
    rows = pl.BlockSpec((PEER_HEADS, ab, PEER_TM), lambda i, j: (0, j, i))
    table = pl.BlockSpec((PEER_HEADS * PEER_KEYS, PEER_TM), lambda i, j: (0, i), pipeline_mode=pl.Buffered(1))
    return pl.pallas_call(
        _peer_dense_kernel,
        grid=(TOKENS // PEER_TM, PEER_EXPERTS // PEER_TE),
        in_specs=[pl.BlockSpec((D_MODEL, PEER_TM), lambda i, j: (0, i), pipeline_mode=pl.Buffered(1)),
                  table, table, rows, rows,
                  pl.BlockSpec((None, PEER_TE, D_MODEL), lambda i, j: (layer, j, 0)),
                  pl.BlockSpec((None, PEER_TE, D_MODEL), lambda i, j: (layer, j, 0))],
        out_specs=pl.BlockSpec((D_MODEL, PEER_TM), lambda i, j: (0, i)),
        out_shape=jax.ShapeDtypeStruct((D_MODEL, TOKENS), F32),
        scratch_shapes=[pltpu.VMEM((PEER_TE, PEER_TM), BF16),
                        pltpu.VMEM((PEER_HEADS * PEER_KEYS, PEER_TM), BF16),
                        pltpu.VMEM((PEER_HEADS * PEER_KEYS, PEER_TM), BF16)],
        compiler_params=_params(("parallel", "arbitrary")),
        name="peer_dense",
    )(ht, eb, rb, ea, cnt, u, v)


def _peer_resid_kernel(x_ref, ot_ref, g_ref, gain_ref, sh_ref, sc_ref, x2_ref, h_ref):
    x2 = x_ref[...] + g_ref[0] * ot_ref[...].T
    x2_ref[...] = x2
    h_ref[...] = _rms_mod(x2, gain_ref[...], sc_ref[0], sh_ref[0]).astype(BF16)


def _final_kernel(x_ref, ot_ref, g_ref, gain_ref, o_ref):
    x2 = x_ref[...] + g_ref[0] * ot_ref[...].T
    o_ref[...] = x2 * lax.rsqrt(jnp.mean(x2 * x2, axis=-1, keepdims=True) + EPS) * gain_ref[...]


def peer_residual(x1, out_t, mod, layer, gain_next):
    tm = 256
    full = pl.BlockSpec((tm, D_MODEL), lambda i: (i, 0))
    return pl.pallas_call(
        _peer_resid_kernel,
        grid=(TOKENS // tm,),
        in_specs=[full, pl.BlockSpec((D_MODEL, tm), lambda i: (0, i)), _mod_spec(layer, 5, SEQ // tm),
                  pl.BlockSpec((1, D_MODEL), lambda i: (0, 0)),
                  _mod_spec(layer + 1, 0, SEQ // tm), _mod_spec(layer + 1, 1, SEQ // tm)],
        out_specs=[full, full],
        out_shape=[jax.ShapeDtypeStruct((TOKENS, D_MODEL), F32), jax.ShapeDtypeStruct((TOKENS, D_MODEL), BF16)],
        compiler_params=_params(("parallel",)),
        name="peer_residual",
    )(x1, out_t, mod, gain_next.reshape(1, D_MODEL), mod, mod)


def final_residual_norm(x1, out_t, mod, layer, gain):
    tm = 256
    full = pl.BlockSpec((tm, D_MODEL), lambda i: (i, 0))
    return pl.pallas_call(
        _final_kernel,
        grid=(TOKENS // tm,),
        in_specs=[full, pl.BlockSpec((D_MODEL, tm), lambda i: (0, i)), _mod_spec(layer, 5, SEQ // tm),
                  pl.BlockSpec((1, D_MODEL), lambda i: (0, 0))],
        out_specs=full,
        out_shape=jax.ShapeDtypeStruct((TOKENS, D_MODEL), F32),
        compiler_params=_params(("parallel",)),
        name="final_norm",
    )(x1, out_t, mod, gain.reshape(1, D_MODEL))


def _rope_tables(positions):
    inv = ROPE_THETA ** (-jnp.arange(0, 2 * ROT_HALF, 2, dtype=F32) / (2 * ROT_HALF))
    ang = positions.astype(F32).reshape(TOKENS, 1) * inv
    cos, sin = jnp.cos(ang), jnp.sin(ang)
    rest = HEAD_DIM - 2 * ROT_HALF
    cs = jnp.concatenate([cos, cos, jnp.ones((TOKENS, rest), F32)], axis=1)
    sn = jnp.concatenate([-sin, sin, jnp.zeros((TOKENS, rest), F32)], axis=1)
    return cs, sn


def kernel(x, c, positions, norm_mix, norm_ffn, w_ada, b_ada, w_in, w_out, cmp_k_pos, cmp_k_w1, cmp_k_b1, cmp_k_w2, cmp_k_b2, cmp_v_pos, cmp_v_w1, cmp_v_b1, cmp_v_w2, cmp_v_b2, hgrn_norm, hgrn_lb, peer_wq, peer_keys, peer_u, peer_v, final_norm):
    xf = x.reshape(TOKENS, D_MODEL)
    rope_cs, rope_sn = _rope_tables(positions)
    c8 = jnp.concatenate([c, jnp.zeros((8 - BATCH, D_MODEL), F32)], axis=0)
    mod = ada_mod(c8, w_ada, b_ada).reshape(DEPTH * 8 * 6, 1, D_MODEL)
    lb_all = jnp.cumsum(jax.nn.softmax(hgrn_lb.astype(F32), axis=0), axis=0)
    lb_all = lb_all - lb_all[0:1]

    w_nsa, w_hgrn, w_gate = split_w_in(w_in)
    w_out_b = w_out.astype(BF16)
    cmp_pos = jnp.stack([cmp_k_pos, cmp_v_pos], axis=1)
    cmp_w1 = jnp.stack([cmp_k_w1, cmp_v_w1], axis=1).astype(BF16)
    cmp_b1 = jnp.stack([cmp_k_b1, cmp_v_b1], axis=1).reshape(DEPTH, 2, 1, CMP_HIDDEN)
    cmp_w2 = jnp.stack([cmp_k_w2, cmp_v_w2], axis=1).astype(BF16)
    cmp_b2 = jnp.stack([cmp_k_b2, cmp_v_b2], axis=1).reshape(DEPTH, 2, 1, HEAD_DIM)
    wq_b = peer_wq.astype(BF16)
    keys = peer_keys.reshape(DEPTH, PEER_HEADS * 2, PEER_KEYS, PEER_KEYS)
    u_b = peer_u.astype(BF16)
    v_b = peer_v.astype(BF16)

    h = adaln(xf, norm_mix[0], mod, 0)
    for l in range(DEPTH):
        p_nsa = in_proj(h, w_nsa, l)
        p_hgrn = in_proj(h, w_hgrn, l)
        p_gate = in_proj(h, w_gate, l)
        cmp_kv = compress(p_nsa, cmp_pos, cmp_w1, cmp_b1, cmp_w2, cmp_b2, l)
        o_nsa = nsa_attention(p_nsa, p_gate, cmp_kv, rope_cs, rope_sn)
        o_hgrn = hgrn(p_hgrn, lb_all[l], hgrn_norm[l])
        x1, ht = out_proj(o_nsa, o_hgrn, w_out_b, xf, norm_ffn[l], mod, l)
        eb, rb, ea, cnt = peer_scores(ht, wq_b, keys, l)
        out_t = peer_dense(ht, eb, rb, ea, cnt, u_b, v_b, l)
        if l + 1 < DEPTH:
            xf, h = peer_residual(x1, out_t, mod, l, norm_mix[l + 1])
        else:
            xf = final_residual_norm(x1, out_t, mod, l, final_norm)
    return xf.reshape(BATCH, SEQ, D_MODEL)
```

```python
import numpy as np
import jax
import jax.numpy as jnp
from jax import lax
from jax.experimental import pallas as pl
from jax.experimental.pallas import tpu as pltpu

F32 = jnp.float32
BF16 = jnp.bfloat16

D_MODEL = 2048
BATCH = 4
SEQ = 2048
TOKENS = BATCH * SEQ
DEPTH = 2
EPS = 1e-6

HEAD_DIM = 128
NSA_HEADS = 8
NSA_GROUPS = 2
NSA_HPG = NSA_HEADS // NSA_GROUPS
CMP_BLOCK = 32
CMP_STRIDE = 16
CMP_HIDDEN = 256
N_CMP = (SEQ - CMP_BLOCK) // CMP_STRIDE + 1
N_CMP_PAD = 128
SLC_BLOCK = 64
N_SLC = SEQ // SLC_BLOCK
SLC_TOPN = 16
N_LOCAL_FORCED = 2
WINDOW = 512
ROT_HALF = 16
ROPE_THETA = 500000.0
Q_TILE = 128
WIN_KEYS = WINDOW + Q_TILE

HGRN_HEADS = 8
HGRN_CHUNK = 64
HGRN_SUB = 16

PEER_HEADS = 8
PEER_KEYS = 128
PEER_EXPERTS = PEER_KEYS * PEER_KEYS
PEER_TOPK = 16

NSA_COLS = 2560
COL_Q, COL_KC, COL_VC, COL_KS, COL_VS, COL_KW, COL_VW = 0, 8, 10, 12, 14, 16, 18
COL_HQ, COL_HF, COL_HI, COL_HG = 0, 8, 16, 24

VMEM_LIMIT = 61 * 1024 * 1024


def _params(sem):
    return pltpu.CompilerParams(dimension_semantics=sem, vmem_limit_bytes=VMEM_LIMIT)


def _ada_kernel(c_ref, w_ref, b_ref, o_ref):
    c = c_ref[...]
    o_ref[0] = jnp.dot(c * jax.nn.sigmoid(c), w_ref[0], preferred_element_type=F32) + b_ref[0]


def ada_mod(c8, w_ada, b_ada):
    tn = 1024
    n = 6 * D_MODEL
    return pl.pallas_call(
        _ada_kernel,
        grid=(DEPTH, n // tn),
        in_specs=[
            pl.BlockSpec((8, D_MODEL), lambda l, j: (0, 0)),
            pl.BlockSpec((1, D_MODEL, tn), lambda l, j: (l, 0, j)),
            pl.BlockSpec((1, 1, tn), lambda l, j: (l, 0, j)),
        ],
        out_specs=pl.BlockSpec((1, 8, tn), lambda l, j: (l, 0, j)),
        out_shape=jax.ShapeDtypeStruct((DEPTH, 8, n), F32),
        compiler_params=_params(("parallel", "parallel")),
        name="ada_mod",
    )(c8, w_ada, b_ada.reshape(DEPTH, 1, n))


def _mod_spec(layer, k, rows_per_batch):
    return pl.BlockSpec((1, 1, D_MODEL), lambda i: ((layer * 8 + i // rows_per_batch) * 6 + k, 0, 0))


def _rms_mod(x, gain, scale, shift):
    y = x * lax.rsqrt(jnp.mean(x * x, axis=-1, keepdims=True) + EPS) * gain
    return y * (1.0 + scale) + shift


def _adaln_kernel(x_ref, gain_ref, sh_ref, sc_ref, h_ref):
    h_ref[...] = _rms_mod(x_ref[...], gain_ref[...], sc_ref[0], sh_ref[0]).astype(BF16)


def adaln(x, gain, mod, layer):
    tm = 256
    return pl.pallas_call(
        _adaln_kernel,
        grid=(TOKENS // tm,),
        in_specs=[
            pl.BlockSpec((tm, D_MODEL), lambda i: (i, 0)),
            pl.BlockSpec((1, D_MODEL), lambda i: (0, 0)),
            _mod_spec(layer, 0, SEQ // tm),
            _mod_spec(layer, 1, SEQ // tm),
        ],
        out_specs=pl.BlockSpec((tm, D_MODEL), lambda i: (i, 0)),
        out_shape=jax.ShapeDtypeStruct((TOKENS, D_MODEL), BF16),
        compiler_params=_params(("parallel",)),
        name="adaln",
    )(x, gain.reshape(1, D_MODEL), mod, mod)


def _matmul_nt_kernel(a_ref, bt_ref, o_ref):
    o_ref[...] = lax.dot_general(a_ref[...], bt_ref[...], (((1,), (1,)), ((), ())), preferred_element_type=F32)


def in_proj(h, wt, layer):
    n = wt.shape[1]
    tm = 1024
    tn = next(t for t in (1280, 1024, 256) if n % t == 0)
    return pl.pallas_call(
        _matmul_nt_kernel,
        grid=(TOKENS // tm, n // tn),
        in_specs=[
            pl.BlockSpec((tm, D_MODEL), lambda i, j: (i, 0)),
            pl.BlockSpec((None, tn, D_MODEL), lambda i, j: (layer, j, 0)),
        ],
        out_specs=pl.BlockSpec((tm, tn), lambda i, j: (i, j)),
        out_shape=jax.ShapeDtypeStruct((TOKENS, n), F32),
        compiler_params=_params(("parallel", "parallel")),
        name="in_proj",
    )(h, wt)


def split_w_in(w):
    wt = jnp.swapaxes(w, 1, 2)
    gl = wt[:, NSA_COLS:NSA_COLS + 24]
    z = jnp.zeros((w.shape[0], 116, D_MODEL), w.dtype)
    gates = jnp.concatenate([gl[:, :12], z, gl[:, 12:], z], axis=1)
    return wt[:, :NSA_COLS].astype(BF16), wt[:, NSA_COLS + 24:].astype(BF16), gates.astype(BF16)


def _compress_kernel(t_ref, pos_ref, w1_ref, b1_ref, w2_ref, b2_ref, o_ref, pad_ref):
    pad_ref[0:SEQ, :] = t_ref[...]
    pad_ref[SEQ:SEQ + CMP_BLOCK, :] = jnp.zeros((CMP_BLOCK, HEAD_DIM), F32)
    acc = jnp.zeros((N_CMP_PAD, CMP_HIDDEN), F32)
    for l in range(CMP_BLOCK):
        tl = pad_ref[pl.ds(l, N_CMP_PAD, stride=CMP_STRIDE), :] + pos_ref[0, l:l + 1, :]
        acc = acc + jnp.dot(tl.astype(BF16), w1_ref[0, l * HEAD_DIM:(l + 1) * HEAD_DIM, :],
                            preferred_element_type=F32)
    hid = jax.nn.gelu(acc + b1_ref[0])
    o_ref[0, 0] = jnp.dot(hid.astype(BF16), w2_ref[0], preferred_element_type=F32) + b2_ref[0]


def compress(proj, pos, w1, b1, w2, b2, layer):
    return pl.pallas_call(
        _compress_kernel,
        grid=(2, BATCH, NSA_GROUPS),
        in_specs=[
            pl.BlockSpec((SEQ, HEAD_DIM), lambda kv, b, g: (b, COL_KC + 2 * kv + g)),
            pl.BlockSpec((None, 1, CMP_BLOCK, HEAD_DIM), lambda kv, b, g: (layer, kv, 0, 0)),
            pl.BlockSpec((None, 1, CMP_BLOCK * HEAD_DIM, CMP_HIDDEN), lambda kv, b, g: (layer, kv, 0, 0)),
            pl.BlockSpec((None, 1, 1, CMP_HIDDEN), lambda kv, b, g: (layer, kv, 0, 0)),
            pl.BlockSpec((None, 1, CMP_HIDDEN, HEAD_DIM), lambda kv, b, g: (layer, kv, 0, 0)),
            pl.BlockSpec((None, 1, 1, HEAD_DIM), lambda kv, b, g: (layer, kv, 0, 0)),
        ],
        out_specs=pl.BlockSpec((1, 1, N_CMP_PAD, HEAD_DIM), lambda kv, b, g: (kv, b * NSA_GROUPS + g, 0, 0)),
        out_shape=jax.ShapeDtypeStruct((2, BATCH * NSA_GROUPS, N_CMP_PAD, HEAD_DIM), F32),
        scratch_shapes=[pltpu.VMEM((SEQ + CMP_BLOCK, HEAD_DIM), F32)],
        compiler_params=_params(("arbitrary", "arbitrary", "arbitrary")),
        name="nsa_compress",
    )(proj, pos, w1, b1, w2, b2)


def _rope(t, cs, sn):
    lane = lax.broadcasted_iota(jnp.int32, t.shape, 1)
    swapped = jnp.where(lane < ROT_HALF, pltpu.roll(t, HEAD_DIM - ROT_HALF, axis=1), pltpu.roll(t, ROT_HALF, axis=1))
    return t * cs + swapped * sn


def _dot_nt(a, b):
    return lax.dot_general(a, b, (((1,), (1,)), ((), ())), preferred_element_type=F32)


MASKED = -1e30
GROUP_TILES = 2


def _nsa_kernel(q_ref, kc_ref, vc_ref, ks_ref, vs_ref, kw_ref, vw_ref, gl_ref, cs_ref, sn_ref, wov_ref,
                o_ref, kaug, vaug, kwr, vwaug):
    scale = HEAD_DIM ** -0.5
    neg_inf = -jnp.inf
    row = lax.broadcasted_iota(jnp.int32, (SEQ, HEAD_DIM), 0)
    lane = lax.broadcasted_iota(jnp.int32, (SEQ, HEAD_DIM), 1)
    kaug[:, 0:HEAD_DIM] = _rope(ks_ref[...], cs_ref[...], sn_ref[...]).astype(BF16)
    kaug[:, HEAD_DIM:2 * HEAD_DIM] = jnp.where(lane == (row >> 6), 1.0, 0.0).astype(BF16)
    kwr[...] = _rope(kw_ref[...], cs_ref[...], sn_ref[...]).astype(BF16)
    ones = jnp.ones((SEQ, HEAD_DIM), BF16)
    vaug[:, 0:HEAD_DIM] = vs_ref[...].astype(BF16)
    vaug[:, HEAD_DIM:2 * HEAD_DIM] = ones
    vwaug[:, 0:HEAD_DIM] = vw_ref[...].astype(BF16)
    vwaug[:, HEAD_DIM:2 * HEAD_DIM] = ones
    kcb = kc_ref[0, 0].astype(BF16)
    vcb = vc_ref[0, 0].astype(BF16)

    def tile(i, extent):
        t0 = pl.multiple_of(i * Q_TILE, Q_TILE)
        q4 = q_ref[pl.ds(t0, Q_TILE), :] * scale
        cs = cs_ref[pl.ds(t0, Q_TILE), :]
        sn = sn_ref[pl.ds(t0, Q_TILE), :]
        gate = jax.nn.sigmoid(gl_ref[pl.ds(t0, Q_TILE), :])
        qh = [q4[:, h * HEAD_DIM:(h + 1) * HEAD_DIM] for h in range(NSA_HPG)]

        qs = jnp.concatenate(qh, axis=0).astype(BF16)
        sc = _dot_nt(qs, kcb)
        rows = lax.broadcasted_iota(jnp.int32, sc.shape, 0)
        ncol = lax.broadcasted_iota(jnp.int32, sc.shape, 1)
        tok = t0 + (rows & (Q_TILE - 1))
        valid = ncol * CMP_STRIDE + (CMP_BLOCK - 1) <= tok
        s = jnp.where(valid, sc, neg_inf)
        m = jnp.max(s, axis=-1, keepdims=True)
        m = jnp.where(m == neg_inf, 0.0, m)
        e = jnp.where(valid, jnp.exp(s - m), 0.0)
        p = e / jnp.maximum(jnp.sum(e, axis=-1, keepdims=True), 1e-30)
        o_cmp = jnp.dot(p.astype(BF16), vcb, preferred_element_type=F32)

        p4 = p[0:Q_TILE] + p[Q_TILE:2 * Q_TILE] + p[2 * Q_TILE:3 * Q_TILE] + p[3 * Q_TILE:4 * Q_TILE]
        imp = jnp.dot(p4, wov_ref[...], preferred_element_type=F32)
        jj = lax.broadcasted_iota(jnp.int32, imp.shape, 1)
        tt = t0 + lax.broadcasted_iota(jnp.int32, imp.shape, 0)
        back = (tt >> 6) - jj
        forced = (jj == 0) | (back.astype(jnp.uint32) < N_LOCAL_FORCED)
        score = jnp.where(forced, jnp.inf, jnp.where(jj * SLC_BLOCK <= tt, imp, neg_inf))
        score_t = score.T[0:N_SLC]
        jrow = lax.broadcasted_iota(jnp.int32, score_t.shape, 0)
        rank = jnp.zeros(score_t.shape, F32)
        for jp in range(N_SLC):
            other = score_t[jp:jp + 1]
            later = jnp.where(jrow > jp, 1.0, 0.0)
            rank = rank + jnp.where(other > score_t, 1.0, jnp.where(other == score_t, later, 0.0))
        bias_t = jnp.where(rank < SLC_TOPN, 0.0, MASKED)
        sel_bias = jnp.concatenate([bias_t, jnp.zeros((HEAD_DIM - N_SLC, Q_TILE), F32)], axis=0).T

        tail = GROUP_TILES * Q_TILE
        rows4 = NSA_HPG * Q_TILE
        kpos = (extent - tail) + lax.broadcasted_iota(jnp.int32, (rows4, tail), 1)
        ok_tail = kpos <= t0 + (lax.broadcasted_iota(jnp.int32, (rows4, tail), 0) & (Q_TILE - 1))

        qr = jnp.concatenate([_rope(qh[h], cs, sn) for h in range(NSA_HPG)], axis=0)
        qa = jnp.concatenate([qr, jnp.concatenate([sel_bias] * NSA_HPG, axis=0)], axis=1).astype(BF16)
        s1 = _dot_nt(qa, kaug[0:extent, :])
        s1_tail = jnp.where(ok_tail, s1[:, extent - tail:], MASKED)
        s1 = s1_tail if extent == tail else jnp.concatenate([s1[:, :extent - tail], s1_tail], axis=1)
        e1 = jnp.exp(s1 - jnp.max(s1, axis=-1, keepdims=True)).astype(BF16)
        r1 = jnp.dot(e1, vaug[0:extent, :], preferred_element_type=F32)
        o_slc = r1[:, :HEAD_DIM] / r1[:, HEAD_DIM:HEAD_DIM + 1]

        ws = pl.multiple_of(jnp.clip(t0 - WINDOW, 0, SEQ - WIN_KEYS), Q_TILE)
        wk = ws + lax.broadcasted_iota(jnp.int32, (rows4, WIN_KEYS), 1)
        wt = t0 + (lax.broadcasted_iota(jnp.int32, (rows4, WIN_KEYS), 0) & (Q_TILE - 1))
        ok_win = (wt - wk).astype(jnp.uint32) < WINDOW
        s2 = jnp.where(ok_win, _dot_nt(qr.astype(BF16), kwr[pl.ds(ws, WIN_KEYS), :]), MASKED)
        e2 = jnp.exp(s2 - jnp.max(s2, axis=-1, keepdims=True)).astype(BF16)
        r2 = jnp.dot(e2, vwaug[pl.ds(ws, WIN_KEYS), :], preferred_element_type=F32)
        o_win = r2[:, :HEAD_DIM] / r2[:, HEAD_DIM:HEAD_DIM + 1]

        outs = []
        for h in range(NSA_HPG):
            hr = slice(h * Q_TILE, (h + 1) * Q_TILE)
            outs.append(gate[:, 3 * h:3 * h + 1] * o_cmp[hr] + gate[:, 3 * h + 1:3 * h + 2] * o_slc[hr]
                        + gate[:, 3 * h + 2:3 * h + 3] * o_win[hr])
        o_ref[pl.ds(t0, Q_TILE), :] = jnp.concatenate(outs, axis=1).astype(BF16)

    for grp in range(SEQ // (GROUP_TILES * Q_TILE)):
        extent = (grp + 1) * GROUP_TILES * Q_TILE

        def body(ii, carry, grp=grp, extent=extent):
            tile(grp * GROUP_TILES + ii, extent)
            return carry

        lax.fori_loop(0, GROUP_TILES, body, 0)


def _cmp_to_slc_weights():
    cs = np.arange(N_CMP) * CMP_STRIDE
    ce = cs + CMP_BLOCK
    ss = np.arange(N_SLC) * SLC_BLOCK
    se = ss + SLC_BLOCK
    ov = np.clip(np.minimum(ce[:, None], se[None, :]) - np.maximum(cs[:, None], ss[None, :]), 0, None)
    w = np.zeros((N_CMP_PAD, HEAD_DIM), np.float32)
    w[:N_CMP, :N_SLC] = ov / CMP_BLOCK
    return w


def nsa_attention(proj, gates, cmp_kv, rope_cs, rope_sn):
    gw = NSA_HPG * HEAD_DIM
    col = lambda c: pl.BlockSpec((SEQ, HEAD_DIM), lambda b, g: (b, c + g))
    return pl.pallas_call(
        _nsa_kernel,
        grid=(BATCH, NSA_GROUPS),
        in_specs=[
            pl.BlockSpec((SEQ, gw), lambda b, g: (b, g)),
            pl.BlockSpec((1, 1, N_CMP_PAD, HEAD_DIM), lambda b, g: (0, b * NSA_GROUPS + g, 0, 0)),
            pl.BlockSpec((1, 1, N_CMP_PAD, HEAD_DIM), lambda b, g: (1, b * NSA_GROUPS + g, 0, 0)),
            col(COL_KS), col(COL_VS), col(COL_KW), col(COL_VW), col(0),
            pl.BlockSpec((SEQ, HEAD_DIM), lambda b, g: (b, 0)),
            pl.BlockSpec((SEQ, HEAD_DIM), lambda b, g: (b, 0)),
            pl.BlockSpec((N_CMP_PAD, HEAD_DIM), lambda b, g: (0, 0)),
        ],
        out_specs=pl.BlockSpec((SEQ, gw), lambda b, g: (b, g)),
        out_shape=jax.ShapeDtypeStruct((TOKENS, NSA_HEADS * HEAD_DIM), BF16),
        scratch_shapes=[pltpu.VMEM((SEQ, 2 * HEAD_DIM), BF16), pltpu.VMEM((SEQ, 2 * HEAD_DIM), BF16),
                        pltpu.VMEM((SEQ, HEAD_DIM), BF16), pltpu.VMEM((SEQ, 2 * HEAD_DIM), BF16)],
        compiler_params=_params(("parallel", "parallel")),
        name="nsa_attention",
    )(proj, cmp_kv, cmp_kv, proj, proj, proj, proj, gates, rope_cs, rope_sn, jnp.asarray(_cmp_to_slc_weights()))


HGRN_HEADS_PER_STEP = 4


def _hgrn_chunk(qc, fc, vc, gc, lb, log_lb, log_1m, gn, tri, state_t):
    C, SUB = HGRN_CHUNK, HGRN_SUB
    row8 = lax.broadcasted_iota(jnp.int32, (8, HEAD_DIM), 0)
    qh = qc * jax.nn.sigmoid(qc)
    log_sig = jnp.minimum(fc, 0.0) - jnp.log(1.0 + jnp.exp(-jnp.abs(fc)))
    b = log_1m + log_sig
    lf = jnp.maximum(log_lb, b) + jnp.log(1.0 + jnp.exp(-jnp.abs(log_lb - b)))
    kh = (1.0 - lb) * jax.nn.sigmoid(-fc)
    a = jnp.dot(tri, lf, preferred_element_type=F32)
    inter = _dot_nt(qh * jnp.exp(a), state_t)
    parts = []
    for blk in range(C // SUB):
        lo = blk * SUB
        ab, qb, kb, vb = a[lo:lo + SUB], qh[lo:lo + SUB], kh[lo:lo + SUB], vc[lo:lo + SUB]
        acc = inter[lo:lo + SUB]
        if blk > 0:
            a_ref_row = a[lo - 1:lo]
            qt = qb * jnp.exp(ab - a_ref_row)
            kt = kh[0:lo] * jnp.exp(a_ref_row - a[0:lo])
            acc = acc + jnp.dot(_dot_nt(qt, kt), vc[0:lo], preferred_element_type=F32)
        for r0 in range(0, SUB, 8):
            ar, qr, acc_r = ab[r0:r0 + 8], qb[r0:r0 + 8], acc[r0:r0 + 8]
            for s in range(r0 + 8):
                rel = ar - ab[s:s + 1]
                if s >= r0:
                    rel = jnp.where(row8 >= s - r0, rel, -jnp.inf)
                w = jnp.sum(qr * kb[s:s + 1] * jnp.exp(rel), axis=-1, keepdims=True)
                acc_r = acc_r + w * vb[s:s + 1]
            parts.append(acc_r)
    o = jnp.concatenate(parts, axis=0)
    a_end = a[C - 1:C]
    kt_end = kh * jnp.exp(a_end - a)
    state_t = state_t * jnp.exp(a_end) + lax.dot_general(vc, kt_end, (((0,), (0,)), ((), ())),
                                                          preferred_element_type=F32)
    o = o * lax.rsqrt(jnp.mean(o * o, axis=-1, keepdims=True) + EPS) * gn
    return (o * jax.nn.sigmoid(gc)).astype(BF16), state_t


def _hgrn_kernel(q_ref, f_ref, i_ref, g_ref, lb_ref, gn_ref, tri_ref, o_ref):
    C, HP = HGRN_CHUNK, HGRN_HEADS_PER_STEP
    lbs = [lb_ref[hp] for hp in range(HP)]
    log_lbs = [jnp.log(lb) for lb in lbs]
    log_1ms = [jnp.log(1.0 - lb) for lb in lbs]
    gns = [gn_ref[hp] for hp in range(HP)]

    def chunk(c, states):
        r0 = pl.multiple_of(c * C, C)
        qc = q_ref[pl.ds(r0, C), :]
        fc = f_ref[pl.ds(r0, C), :]
        vc = i_ref[pl.ds(r0, C), :]
        gc = g_ref[pl.ds(r0, C), :]
        outs, new_states = [], []
        for hp in range(HP):
            cols = slice(hp * HEAD_DIM, (hp + 1) * HEAD_DIM)
            o, st = _hgrn_chunk(qc[:, cols], fc[:, cols], vc[:, cols], gc[:, cols], lbs[hp], log_lbs[hp], log_1ms[hp],
                                gns[hp], tri_ref[...], states[hp])
            outs.append(o)
            new_states.append(st)
        o_ref[pl.ds(r0, C), :] = jnp.concatenate(outs, axis=1)
        return tuple(new_states)

    lax.fori_loop(0, SEQ // C, chunk, tuple(jnp.zeros((HEAD_DIM, HEAD_DIM), F32) for _ in range(HP)), unroll=4)


def hgrn(proj, lb, gain):
    hp = HGRN_HEADS_PER_STEP
    col = lambda c: pl.BlockSpec((SEQ, hp * HEAD_DIM), lambda b, h: (b, c // hp + h))
    vec = pl.BlockSpec((hp, 1, HEAD_DIM), lambda b, h: (h, 0, 0))
    tri = np.tril(np.ones((HGRN_CHUNK, HGRN_CHUNK), np.float32))
    return pl.pallas_call(
        _hgrn_kernel,
        grid=(BATCH, HGRN_HEADS // hp),
        in_specs=[col(COL_HQ), col(COL_HF), col(COL_HI), col(COL_HG), vec, vec,
                  pl.BlockSpec((HGRN_CHUNK, HGRN_CHUNK), lambda b, h: (0, 0))],
        out_specs=pl.BlockSpec((SEQ, hp * HEAD_DIM), lambda b, h: (b, h)),
        out_shape=jax.ShapeDtypeStruct((TOKENS, HGRN_HEADS * HEAD_DIM), BF16),
        compiler_params=_params(("parallel", "parallel")),
        name="hgrn2",
    )(proj, proj, proj, proj, lb.reshape(HGRN_HEADS, 1, HEAD_DIM), gain.reshape(HGRN_HEADS, 1, HEAD_DIM),
      jnp.asarray(tri))


def _out_proj_kernel(on_ref, oh_ref, wn_ref, wh_ref, x_ref, g_ref, gain_ref, sh_ref, sc_ref, x1_ref, ht_ref):
    mix = (jnp.dot(on_ref[...], wn_ref[...], preferred_element_type=F32)
           + jnp.dot(oh_ref[...], wh_ref[...], preferred_element_type=F32))
    x1 = x_ref[...] + g_ref[0] * mix
    x1_ref[...] = x1
    ht_ref[...] = _rms_mod(x1, gain_ref[...], sc_ref[0], sh_ref[0]).T.astype(BF16)


def out_proj(o_nsa, o_hgrn, w_out, x, gain, mod, layer):
    tm = 256
    half = D_MODEL // 2
    rows = pl.BlockSpec((tm, half), lambda i: (i, 0))
    full = pl.BlockSpec((tm, D_MODEL), lambda i: (i, 0))
    return pl.pallas_call(
        _out_proj_kernel,
        grid=(TOKENS // tm,),
        in_specs=[rows, rows,
                  pl.BlockSpec((None, half, D_MODEL), lambda i: (layer, 0, 0)),
                  pl.BlockSpec((None, half, D_MODEL), lambda i: (layer, 1, 0)),
                  full, _mod_spec(layer, 2, SEQ // tm),
                  pl.BlockSpec((1, D_MODEL), lambda i: (0, 0)),
                  _mod_spec(layer, 3, SEQ // tm), _mod_spec(layer, 4, SEQ // tm)],
        out_specs=[full, pl.BlockSpec((D_MODEL, tm), lambda i: (0, i))],
        out_shape=[jax.ShapeDtypeStruct((TOKENS, D_MODEL), F32), jax.ShapeDtypeStruct((D_MODEL, TOKENS), BF16)],
        compiler_params=_params(("parallel",)),
        name="out_proj",
    )(o_nsa, o_hgrn, w_out, w_out, x, mod, gain.reshape(1, D_MODEL), mod, mod)


NOT_TOP = 127.0


def _pack_pairs(x):
    return pltpu.bitcast(x, jnp.uint32)


def _unpack_pairs(x):
    return pltpu.bitcast(x, BF16)


def _top_values(x, k, want_rank=False):
    vals = []
    rank = jnp.full(x.shape, NOT_TOP, F32) if want_rank else None
    for i in range(k):
        m = jnp.max(x, axis=0, keepdims=True)
        vals.append(m)
        hit = x == m
        if want_rank:
            rank = jnp.where(hit, float(i), rank)
        x = jnp.where(hit, -jnp.inf, x)
    return vals, rank


def _dot_tn(a, b):
    return lax.dot_general(a, b, (((0,), (0,)), ((), ())), preferred_element_type=F32)


def _peer_score_kernel(ht_ref, wq_ref, keys_ref, eb_ref, rb_ref, ea_ref, cnt_ref):
    k = PEER_TOPK
    qt = _dot_tn(wq_ref[...], ht_ref[...])
    for h in range(PEER_HEADS):
        sa = jnp.dot(keys_ref[2 * h], qt[(2 * h) * PEER_KEYS:(2 * h + 1) * PEER_KEYS], preferred_element_type=F32)
        sb = jnp.dot(keys_ref[2 * h + 1], qt[(2 * h + 1) * PEER_KEYS:(2 * h + 2) * PEER_KEYS],
                     preferred_element_type=F32)
        top_a, _ = _top_values(sa, k)
        top_b, rank_b = _top_values(sb, k, want_rank=True)
        a16 = jnp.concatenate(top_a, axis=0)
        b16 = jnp.concatenate(top_b, axis=0)
        cand = jnp.concatenate([a16 + top_b[0]] + [a16[0:8] + top_b[j] for j in range(1, 8)]
                               + [top_a[0] + b16[8:16]], axis=0)
        best, _ = _top_values(cand, k)
        z = jnp.zeros_like(best[0])
        for v in best:
            z = z + jnp.exp(v - best[0])
        cnt = jnp.zeros_like(sa)
        for j in range(k):
            cnt = cnt + jnp.where(sa + top_b[j] >= best[k - 1], 1.0, 0.0)
        half = slice(h * PEER_KEYS // 2, (h + 1) * PEER_KEYS // 2)
        eb_ref[half, :] = _pack_pairs(jnp.exp(sb - top_b[0]).astype(BF16))
        rb_ref[half, :] = _pack_pairs(rank_b.astype(BF16))
        ea_ref[h] = jnp.exp(sa - top_a[0]) / z
        cnt_ref[h] = cnt


def peer_scores(ht, wq, keys, layer):
    tm = 256
    rows = PEER_HEADS * PEER_KEYS
    spec2 = pl.BlockSpec((rows // 2, tm), lambda i: (0, i))
    shape2 = jax.ShapeDtypeStruct((rows // 2, TOKENS), jnp.uint32)
    spec3 = pl.BlockSpec((PEER_HEADS, PEER_KEYS, tm), lambda i: (0, 0, i))
    shape3 = jax.ShapeDtypeStruct((PEER_HEADS, PEER_KEYS, TOKENS), F32)
    return pl.pallas_call(
        _peer_score_kernel,
        grid=(TOKENS // tm,),
        in_specs=[pl.BlockSpec((D_MODEL, tm), lambda i: (0, i)),
                  pl.BlockSpec((None, D_MODEL, 2 * rows), lambda i: (layer, 0, 0)),
                  pl.BlockSpec((None, PEER_HEADS * 2, PEER_KEYS, PEER_KEYS), lambda i: (layer, 0, 0, 0))],
        out_specs=[spec2, spec2, spec3, spec3],
        out_shape=[shape2, shape2, shape3, shape3],
        compiler_params=_params(("parallel",)),
        name="peer_scores",
    )(ht, wq, keys)


PEER_TM = 1024
PEER_TE = 1024
PEER_LANES = 128


def _gelu_to_bf16(x):
    c = float(np.sqrt(2.0 / np.pi))
    t = jnp.tanh((x * (c + (c * 0.044715) * (x * x))).astype(BF16))
    return (0.5 * x.astype(BF16)) * (1.0 + t)


def _peer_dense_kernel(xt_ref, eb_ref, rb_ref, ea_ref, cnt_ref, u_ref, v_ref, o_ref, gw_ref):
    @pl.when(pl.program_id(1) == 0)
    def _():
        o_ref[...] = jnp.zeros_like(o_ref)

    def rows_bf16(ref, h, ab, cols):
        tile16 = jnp.broadcast_to(ref[h, ab:ab + 1, cols], (16, PEER_LANES)).astype(BF16)
        return jnp.concatenate([tile16] * (PEER_KEYS // 16), axis=0)

    act = jnp.dot(u_ref[...], xt_ref[...], preferred_element_type=F32)
    for ab in range(PEER_TE // PEER_KEYS):
        rows = slice(ab * PEER_KEYS, (ab + 1) * PEER_KEYS)
        for lt in range(PEER_TM // PEER_LANES):
            cols = slice(lt * PEER_LANES, (lt + 1) * PEER_LANES)
            w = jnp.zeros((PEER_KEYS, PEER_LANES), BF16)
            for h in range(PEER_HEADS):
                hb = slice(h * PEER_KEYS // 2, (h + 1) * PEER_KEYS // 2)
                eb = _unpack_pairs(eb_ref[hb, cols])
                rb = _unpack_pairs(rb_ref[hb, cols])
                cnt = rows_bf16(cnt_ref, h, ab, cols)
                w = w + jnp.where(rb < cnt, eb * rows_bf16(ea_ref, h, ab, cols), 0.0)
            gw = _gelu_to_bf16(act[rows, cols]) * w
            gw_ref[rows, cols] = gw
    o_ref[...] += _dot_tn(v_ref[...], gw_ref[...])


def peer_dense(ht, eb, rb, ea, cnt, u, v, layer):
    ab = PEER_TE // PEER_KEYS
    rows = pl.BlockSpec((PEER_HEADS, ab, PEER_TM), lambda i, j: (0, j, i))
    table = pl.BlockSpec((PEER_HEADS * PEER_KEYS // 2, PEER_TM), lambda i, j: (0, i))
    return pl.pallas_call(
        _peer_dense_kernel,
        grid=(TOKENS // PEER_TM, PEER_EXPERTS // PEER_TE),
        in_specs=[pl.BlockSpec((D_MODEL, PEER_TM), lambda i, j: (0, i)),
                  table, table, rows, rows,
                  pl.BlockSpec((None, PEER_TE, D_MODEL), lambda i, j: (layer, j, 0)),
                  pl.BlockSpec((None, PEER_TE, D_MODEL), lambda i, j: (layer, j, 0))],
        out_specs=pl.BlockSpec((D_MODEL, PEER_TM), lambda i, j: (0, i)),
        out_shape=jax.ShapeDtypeStruct((D_MODEL, TOKENS), F32),
        scratch_shapes=[pltpu.VMEM((PEER_TE, PEER_TM), BF16)],
        compiler_params=_params(("parallel", "arbitrary")),
        name="peer_dense",
    )(ht, eb, rb, ea, cnt, u, v)


def _peer_resid_kernel(x_ref, ot_ref, g_ref, gain_ref, sh_ref, sc_ref, x2_ref, h_ref):
    x2 = x_ref[...] + g_ref[0] * ot_ref[...].T
    x2_ref[...] = x2
    h_ref[...] = _rms_mod(x2, gain_ref[...], sc_ref[0], sh_ref[0]).astype(BF16)


def _final_kernel(x_ref, ot_ref, g_ref, gain_ref, o_ref):
    x2 = x_ref[...] + g_ref[0] * ot_ref[...].T
    o_ref[...] = x2 * lax.rsqrt(jnp.mean(x2 * x2, axis=-1, keepdims=True) + EPS) * gain_ref[...]


def peer_residual(x1, out_t, mod, layer, gain_next):
    tm = 256
    full = pl.BlockSpec((tm, D_MODEL), lambda i: (i, 0))
    return pl.pallas_call(
        _peer_resid_kernel,
        grid=(TOKENS // tm,),
        in_specs=[full, pl.BlockSpec((D_MODEL, tm), lambda i: (0, i)), _mod_spec(layer, 5, SEQ // tm),
                  pl.BlockSpec((1, D_MODEL), lambda i: (0, 0)),
                  _mod_spec(layer + 1, 0, SEQ // tm), _mod_spec(layer + 1, 1, SEQ // tm)],
        out_specs=[full, full],
        out_shape=[jax.ShapeDtypeStruct((TOKENS, D_MODEL), F32), jax.ShapeDtypeStruct((TOKENS, D_MODEL), BF16)],
        compiler_params=_params(("parallel",)),
        name="peer_residual",
    )(x1, out_t, mod, gain_next.reshape(1, D_MODEL), mod, mod)


def final_residual_norm(x1, out_t, mod, layer, gain):
    tm = 256
    full = pl.BlockSpec((tm, D_MODEL), lambda i: (i, 0))
    return pl.pallas_call(
        _final_kernel,
        grid=(TOKENS // tm,),
        in_specs=[full, pl.BlockSpec((D_MODEL, tm), lambda i: (0, i)), _mod_spec(layer, 5, SEQ // tm),
                  pl.BlockSpec((1, D_MODEL), lambda i: (0, 0))],
        out_specs=full,
        out_shape=jax.ShapeDtypeStruct((TOKENS, D_MODEL), F32),
        compiler_params=_params(("parallel",)),
        name="final_norm",
    )(x1, out_t, mod, gain.reshape(1, D_MODEL))


def _rope_tables(positions):
    inv = ROPE_THETA ** (-jnp.arange(0, 2 * ROT_HALF, 2, dtype=F32) / (2 * ROT_HALF))
    ang = positions.astype(F32).reshape(TOKENS, 1) * inv
    cos, sin = jnp.cos(ang), jnp.sin(ang)
    rest = HEAD_DIM - 2 * ROT_HALF
    cs = jnp.concatenate([cos, cos, jnp.ones((TOKENS, rest), F32)], axis=1)
    sn = jnp.concatenate([-sin, sin, jnp.zeros((TOKENS, rest), F32)], axis=1)
    return cs, sn


def kernel(x, c, positions, norm_mix, norm_ffn, w_ada, b_ada, w_in, w_out, cmp_k_pos, cmp_k_w1, cmp_k_b1, cmp_k_w2, cmp_k_b2, cmp_v_pos, cmp_v_w1, cmp_v_b1, cmp_v_w2, cmp_v_b2, hgrn_norm, hgrn_lb, peer_wq, peer_keys, peer_u, peer_v, final_norm):
    xf = x.reshape(TOKENS, D_MODEL)
    rope_cs, rope_sn = _rope_tables(positions)
    c8 = jnp.concatenate([c, jnp.zeros((8 - BATCH, D_MODEL), F32)], axis=0)
    mod = ada_mod(c8, w_ada, b_ada).reshape(DEPTH * 8 * 6, 1, D_MODEL)
    lb_all = jnp.cumsum(jax.nn.softmax(hgrn_lb.astype(F32), axis=0), axis=0)
    lb_all = lb_all - lb_all[0:1]

    w_nsa, w_hgrn, w_gate = split_w_in(w_in)
    w_out_b = w_out.astype(BF16)
    cmp_pos = jnp.stack([cmp_k_pos, cmp_v_pos], axis=1)
    cmp_w1 = jnp.stack([cmp_k_w1, cmp_v_w1], axis=1).astype(BF16)
    cmp_b1 = jnp.stack([cmp_k_b1, cmp_v_b1], axis=1).reshape(DEPTH, 2, 1, CMP_HIDDEN)
    cmp_w2 = jnp.stack([cmp_k_w2, cmp_v_w2], axis=1).astype(BF16)
    cmp_b2 = jnp.stack([cmp_k_b2, cmp_v_b2], axis=1).reshape(DEPTH, 2, 1, HEAD_DIM)
    wq_b = peer_wq.astype(BF16)
    keys = peer_keys.reshape(DEPTH, PEER_HEADS * 2, PEER_KEYS, PEER_KEYS)
    u_b = peer_u.astype(BF16)
    v_b = peer_v.astype(BF16)

    h = adaln(xf, norm_mix[0], mod, 0)
    for l in range(DEPTH):
        p_nsa = in_proj(h, w_nsa, l)
        p_hgrn = in_proj(h, w_hgrn, l)
        p_gate = in_proj(h, w_gate, l)
        cmp_kv = compress(p_nsa, cmp_pos, cmp_w1, cmp_b1, cmp_w2, cmp_b2, l)
        o_nsa = nsa_attention(p_nsa, p_gate, cmp_kv, rope_cs, rope_sn)
        o_hgrn = hgrn(p_hgrn, lb_all[l], hgrn_norm[l])
        x1, ht = out_proj(o_nsa, o_hgrn, w_out_b, xf, norm_ffn[l], mod, l)
        eb, rb, ea, cnt = peer_scores(ht, wq_b, keys, l)
        out_t = peer_dense(ht, eb, rb, ea, cnt, u_b, v_b, l)
        if l + 1 < DEPTH:
            xf, h = peer_residual(x1, out_t, mod, l, norm_mix[l + 1])
        else:
            xf = final_residual_norm(x1, out_t, mod, l, final_norm)
    return xf.reshape(BATCH, SEQ, D_MODEL)
```

```python
import numpy as np
import jax
import jax.numpy as jnp
from jax import lax
from jax.experimental import pallas as pl
from jax.experimental.pallas import tpu as pltpu

F32 = jnp.float32
BF16 = jnp.bfloat16

D_MODEL = 2048
BATCH = 4
SEQ = 2048
TOKENS = BATCH * SEQ
DEPTH = 2
EPS = 1e-6

HEAD_DIM = 128
NSA_HEADS = 8
NSA_GROUPS = 2
NSA_HPG = NSA_HEADS // NSA_GROUPS
CMP_BLOCK = 32
CMP_STRIDE = 16
CMP_HIDDEN = 256
N_CMP = (SEQ - CMP_BLOCK) // CMP_STRIDE + 1
N_CMP_PAD = 128
SLC_BLOCK = 64
N_SLC = SEQ // SLC_BLOCK
SLC_TOPN = 16
N_LOCAL_FORCED = 2
WINDOW = 512
ROT_HALF = 16
ROPE_THETA = 500000.0
Q_TILE = 128
WIN_KEYS = WINDOW + Q_TILE

HGRN_HEADS = 8
HGRN_CHUNK = 64
HGRN_SUB = 16

PEER_HEADS = 8
PEER_KEYS = 128
PEER_EXPERTS = PEER_KEYS * PEER_KEYS
PEER_TOPK = 16

NSA_COLS = 2560
COL_Q, COL_KC, COL_VC, COL_KS, COL_VS, COL_KW, COL_VW = 0, 8, 10, 12, 14, 16, 18
COL_HQ, COL_HF, COL_HI, COL_HG = 0, 8, 16, 24

VMEM_LIMIT = 61 * 1024 * 1024


def _params(sem):
    return pltpu.CompilerParams(dimension_semantics=sem, vmem_limit_bytes=VMEM_LIMIT)


def _ada_kernel(c_ref, w_ref, b_ref, o_ref):
    c = c_ref[...]
    o_ref[0] = jnp.dot(c * jax.nn.sigmoid(c), w_ref[0], preferred_element_type=F32) + b_ref[0]


def ada_mod(c8, w_ada, b_ada):
    tn = 1024
    n = 6 * D_MODEL
    return pl.pallas_call(
        _ada_kernel,
        grid=(DEPTH, n // tn),
        in_specs=[
            pl.BlockSpec((8, D_MODEL), lambda l, j: (0, 0)),
            pl.BlockSpec((1, D_MODEL, tn), lambda l, j: (l, 0, j)),
            pl.BlockSpec((1, 1, tn), lambda l, j: (l, 0, j)),
        ],
        out_specs=pl.BlockSpec((1, 8, tn), lambda l, j: (l, 0, j)),
        out_shape=jax.ShapeDtypeStruct((DEPTH, 8, n), F32),
        compiler_params=_params(("parallel", "parallel")),
        name="ada_mod",
    )(c8, w_ada, b_ada.reshape(DEPTH, 1, n))


def _mod_spec(layer, k, rows_per_batch):
    return pl.BlockSpec((1, 1, D_MODEL), lambda i: ((layer * 8 + i // rows_per_batch) * 6 + k, 0, 0))


def _rms_mod(x, gain, scale, shift):
    y = x * lax.rsqrt(jnp.mean(x * x, axis=-1, keepdims=True) + EPS) * gain
    return y * (1.0 + scale) + shift


def _adaln_kernel(x_ref, gain_ref, sh_ref, sc_ref, h_ref):
    h_ref[...] = _rms_mod(x_ref[...], gain_ref[...], sc_ref[0], sh_ref[0]).astype(BF16)


def adaln(x, gain, mod, layer):
    tm = 256
    return pl.pallas_call(
        _adaln_kernel,
        grid=(TOKENS // tm,),
        in_specs=[
            pl.BlockSpec((tm, D_MODEL), lambda i: (i, 0)),
            pl.BlockSpec((1, D_MODEL), lambda i: (0, 0)),
            _mod_spec(layer, 0, SEQ // tm),
            _mod_spec(layer, 1, SEQ // tm),
        ],
        out_specs=pl.BlockSpec((tm, D_MODEL), lambda i: (i, 0)),
        out_shape=jax.ShapeDtypeStruct((TOKENS, D_MODEL), BF16),
        compiler_params=_params(("parallel",)),
        name="adaln",
    )(x, gain.reshape(1, D_MODEL), mod, mod)


def _matmul_nt_kernel(a_ref, bt_ref, o_ref):
    o_ref[...] = lax.dot_general(a_ref[...], bt_ref[...], (((1,), (1,)), ((), ())), preferred_element_type=F32)


def in_proj(h, wt, layer):
    n = wt.shape[1]
    tm = 1024
    tn = next(t for t in (1280, 1024, 256) if n % t == 0)
    return pl.pallas_call(
        _matmul_nt_kernel,
        grid=(TOKENS // tm, n // tn),
        in_specs=[
            pl.BlockSpec((tm, D_MODEL), lambda i, j: (i, 0)),
            pl.BlockSpec((None, tn, D_MODEL), lambda i, j: (layer, j, 0)),
        ],
        out_specs=pl.BlockSpec((tm, tn), lambda i, j: (i, j)),
        out_shape=jax.ShapeDtypeStruct((TOKENS, n), F32),
        compiler_params=_params(("parallel", "parallel")),
        name="in_proj",
    )(h, wt)


def split_w_in(w):
    wt = jnp.swapaxes(w, 1, 2)
    gl = wt[:, NSA_COLS:NSA_COLS + 24]
    z = jnp.zeros((w.shape[0], 116, D_MODEL), w.dtype)
    gates = jnp.concatenate([gl[:, :12], z, gl[:, 12:], z], axis=1)
    return wt[:, :NSA_COLS].astype(BF16), wt[:, NSA_COLS + 24:].astype(BF16), gates.astype(BF16)


def _compress_kernel(t_ref, pos_ref, w1_ref, b1_ref, w2_ref, b2_ref, o_ref, pad_ref):
    pad_ref[0:SEQ, :] = t_ref[...]
    pad_ref[SEQ:SEQ + CMP_BLOCK, :] = jnp.zeros((CMP_BLOCK, HEAD_DIM), F32)
    acc = jnp.zeros((N_CMP_PAD, CMP_HIDDEN), F32)
    for l in range(CMP_BLOCK):
        tl = pad_ref[pl.ds(l, N_CMP_PAD, stride=CMP_STRIDE), :] + pos_ref[0, l:l + 1, :]
        acc = acc + jnp.dot(tl.astype(BF16), w1_ref[0, l * HEAD_DIM:(l + 1) * HEAD_DIM, :],
                            preferred_element_type=F32)
    hid = jax.nn.gelu(acc + b1_ref[0])
    o_ref[0, 0] = jnp.dot(hid.astype(BF16), w2_ref[0], preferred_element_type=F32) + b2_ref[0]


def compress(proj, pos, w1, b1, w2, b2, layer):
    return pl.pallas_call(
        _compress_kernel,
        grid=(2, BATCH, NSA_GROUPS),
        in_specs=[
            pl.BlockSpec((SEQ, HEAD_DIM), lambda kv, b, g: (b, COL_KC + 2 * kv + g)),
            pl.BlockSpec((None, 1, CMP_BLOCK, HEAD_DIM), lambda kv, b, g: (layer, kv, 0, 0)),
            pl.BlockSpec((None, 1, CMP_BLOCK * HEAD_DIM, CMP_HIDDEN), lambda kv, b, g: (layer, kv, 0, 0)),
            pl.BlockSpec((None, 1, 1, CMP_HIDDEN), lambda kv, b, g: (layer, kv, 0, 0)),
            pl.BlockSpec((None, 1, CMP_HIDDEN, HEAD_DIM), lambda kv, b, g: (layer, kv, 0, 0)),
            pl.BlockSpec((None, 1, 1, HEAD_DIM), lambda kv, b, g: (layer, kv, 0, 0)),
        ],
        out_specs=pl.BlockSpec((1, 1, N_CMP_PAD, HEAD_DIM), lambda kv, b, g: (kv, b * NSA_GROUPS + g, 0, 0)),
        out_shape=jax.ShapeDtypeStruct((2, BATCH * NSA_GROUPS, N_CMP_PAD, HEAD_DIM), F32),
        scratch_shapes=[pltpu.VMEM((SEQ + CMP_BLOCK, HEAD_DIM), F32)],
        compiler_params=_params(("arbitrary", "arbitrary", "arbitrary")),
        name="nsa_compress",
    )(proj, pos, w1, b1, w2, b2)


def _rope(t, cs, sn):
    lane = lax.broadcasted_iota(jnp.int32, t.shape, 1)
    swapped = jnp.where(lane < ROT_HALF, pltpu.roll(t, HEAD_DIM - ROT_HALF, axis=1), pltpu.roll(t, ROT_HALF, axis=1))
    return t * cs + swapped * sn


def _dot_nt(a, b):
    return lax.dot_general(a, b, (((1,), (1,)), ((), ())), preferred_element_type=F32)


MASKED = -1e30
GROUP_TILES = 2


def _nsa_kernel(q_ref, kc_ref, vc_ref, ks_ref, vs_ref, kw_ref, vw_ref, gl_ref, cs_ref, sn_ref, wov_ref,
                o_ref, kaug, vaug, kwr, vwaug):
    scale = HEAD_DIM ** -0.5
    neg_inf = -jnp.inf
    row = lax.broadcasted_iota(jnp.int32, (SEQ, HEAD_DIM), 0)
    lane = lax.broadcasted_iota(jnp.int32, (SEQ, HEAD_DIM), 1)
    kaug[:, 0:HEAD_DIM] = _rope(ks_ref[...], cs_ref[...], sn_ref[...]).astype(BF16)
    kaug[:, HEAD_DIM:2 * HEAD_DIM] = jnp.where(lane == (row >> 6), 1.0, 0.0).astype(BF16)
    kwr[...] = _rope(kw_ref[...], cs_ref[...], sn_ref[...]).astype(BF16)
    ones = jnp.ones((SEQ, HEAD_DIM), BF16)
    vaug[:, 0:HEAD_DIM] = vs_ref[...].astype(BF16)
    vaug[:, HEAD_DIM:2 * HEAD_DIM] = ones
    vwaug[:, 0:HEAD_DIM] = vw_ref[...].astype(BF16)
    vwaug[:, HEAD_DIM:2 * HEAD_DIM] = ones
    kcb = kc_ref[0, 0].astype(BF16)
    vcb = vc_ref[0, 0].astype(BF16)

    def tile(i, extent):
        t0 = pl.multiple_of(i * Q_TILE, Q_TILE)
        q4 = q_ref[pl.ds(t0, Q_TILE), :] * scale
        cs = cs_ref[pl.ds(t0, Q_TILE), :]
        sn = sn_ref[pl.ds(t0, Q_TILE), :]
        gate = jax.nn.sigmoid(gl_ref[pl.ds(t0, Q_TILE), :])
        qh = [q4[:, h * HEAD_DIM:(h + 1) * HEAD_DIM] for h in range(NSA_HPG)]

        qs = jnp.concatenate(qh, axis=0).astype(BF16)
        sc = _dot_nt(qs, kcb)
        rows = lax.broadcasted_iota(jnp.int32, sc.shape, 0)
        ncol = lax.broadcasted_iota(jnp.int32, sc.shape, 1)
        tok = t0 + (rows & (Q_TILE - 1))
        valid = ncol * CMP_STRIDE + (CMP_BLOCK - 1) <= tok
        s = jnp.where(valid, sc, neg_inf)
        m = jnp.max(s, axis=-1, keepdims=True)
        m = jnp.where(m == neg_inf, 0.0, m)
        e = jnp.where(valid, jnp.exp(s - m), 0.0)
        p = e / jnp.maximum(jnp.sum(e, axis=-1, keepdims=True), 1e-30)
        o_cmp = jnp.dot(p.astype(BF16), vcb, preferred_element_type=F32)

        p4 = p[0:Q_TILE] + p[Q_TILE:2 * Q_TILE] + p[2 * Q_TILE:3 * Q_TILE] + p[3 * Q_TILE:4 * Q_TILE]
        imp = jnp.dot(p4, wov_ref[...], preferred_element_type=F32)
        jj = lax.broadcasted_iota(jnp.int32, imp.shape, 1)
        tt = t0 + lax.broadcasted_iota(jnp.int32, imp.shape, 0)
        back = (tt >> 6) - jj
        forced = (jj == 0) | (back.astype(jnp.uint32) < N_LOCAL_FORCED)
        score = jnp.where(forced, jnp.inf, jnp.where(jj * SLC_BLOCK <= tt, imp, neg_inf))
        score_t = score.T[0:N_SLC]
        jrow = lax.broadcasted_iota(jnp.int32, score_t.shape, 0)
        rank = jnp.zeros(score_t.shape, F32)
        for jp in range(N_SLC):
            other = score_t[jp:jp + 1]
            later = jnp.where(jrow > jp, 1.0, 0.0)
            rank = rank + jnp.where(other > score_t, 1.0, jnp.where(other == score_t, later, 0.0))
        bias_t = jnp.where(rank < SLC_TOPN, 0.0, MASKED)
        sel_bias = jnp.concatenate([bias_t, jnp.zeros((HEAD_DIM - N_SLC, Q_TILE), F32)], axis=0).T

        tail = GROUP_TILES * Q_TILE
        rows4 = NSA_HPG * Q_TILE
        kpos = (extent - tail) + lax.broadcasted_iota(jnp.int32, (rows4, tail), 1)
        ok_tail = kpos <= t0 + (lax.broadcasted_iota(jnp.int32, (rows4, tail), 0) & (Q_TILE - 1))

        qr = jnp.concatenate([_rope(qh[h], cs, sn) for h in range(NSA_HPG)], axis=0)
        qa = jnp.concatenate([qr, jnp.concatenate([sel_bias] * NSA_HPG, axis=0)], axis=1).astype(BF16)
        s1 = _dot_nt(qa, kaug[0:extent, :])
        s1_tail = jnp.where(ok_tail, s1[:, extent - tail:], MASKED)
        s1 = s1_tail if extent == tail else jnp.concatenate([s1[:, :extent - tail], s1_tail], axis=1)
        e1 = jnp.exp(s1 - jnp.max(s1, axis=-1, keepdims=True)).astype(BF16)
        r1 = jnp.dot(e1, vaug[0:extent, :], preferred_element_type=F32)
        o_slc = r1[:, :HEAD_DIM] / r1[:, HEAD_DIM:HEAD_DIM + 1]

        ws = pl.multiple_of(jnp.clip(t0 - WINDOW, 0, SEQ - WIN_KEYS), Q_TILE)
        wk = ws + lax.broadcasted_iota(jnp.int32, (rows4, WIN_KEYS), 1)
        wt = t0 + (lax.broadcasted_iota(jnp.int32, (rows4, WIN_KEYS), 0) & (Q_TILE - 1))
        ok_win = (wt - wk).astype(jnp.uint32) < WINDOW
        s2 = jnp.where(ok_win, _dot_nt(qr.astype(BF16), kwr[pl.ds(ws, WIN_KEYS), :]), MASKED)
        e2 = jnp.exp(s2 - jnp.max(s2, axis=-1, keepdims=True)).astype(BF16)
        r2 = jnp.dot(e2, vwaug[pl.ds(ws, WIN_KEYS), :], preferred_element_type=F32)
        o_win = r2[:, :HEAD_DIM] / r2[:, HEAD_DIM:HEAD_DIM + 1]

        outs = []
        for h in range(NSA_HPG):
            hr = slice(h * Q_TILE, (h + 1) * Q_TILE)
            outs.append(gate[:, 3 * h:3 * h + 1] * o_cmp[hr] + gate[:, 3 * h + 1:3 * h + 2] * o_slc[hr]
                        + gate[:, 3 * h + 2:3 * h + 3] * o_win[hr])
        o_ref[pl.ds(t0, Q_TILE), :] = jnp.concatenate(outs, axis=1).astype(BF16)

    for grp in range(SEQ // (GROUP_TILES * Q_TILE)):
        extent = (grp + 1) * GROUP_TILES * Q_TILE

        def body(ii, carry, grp=grp, extent=extent):
            tile(grp * GROUP_TILES + ii, extent)
            return carry

        lax.fori_loop(0, GROUP_TILES, body, 0)


def _cmp_to_slc_weights():
    cs = np.arange(N_CMP) * CMP_STRIDE
    ce = cs + CMP_BLOCK
    ss = np.arange(N_SLC) * SLC_BLOCK
    se = ss + SLC_BLOCK
    ov = np.clip(np.minimum(ce[:, None], se[None, :]) - np.maximum(cs[:, None], ss[None, :]), 0, None)
    w = np.zeros((N_CMP_PAD, HEAD_DIM), np.float32)
    w[:N_CMP, :N_SLC] = ov / CMP_BLOCK
    return w


def nsa_attention(proj, gates, cmp_kv, rope_cs, rope_sn):
    gw = NSA_HPG * HEAD_DIM
    col = lambda c: pl.BlockSpec((SEQ, HEAD_DIM), lambda b, g: (b, c + g))
    return pl.pallas_call(
        _nsa_kernel,
        grid=(BATCH, NSA_GROUPS),
        in_specs=[
            pl.BlockSpec((SEQ, gw), lambda b, g: (b, g)),
            pl.BlockSpec((1, 1, N_CMP_PAD, HEAD_DIM), lambda b, g: (0, b * NSA_GROUPS + g, 0, 0)),
            pl.BlockSpec((1, 1, N_CMP_PAD, HEAD_DIM), lambda b, g: (1, b * NSA_GROUPS + g, 0, 0)),
            col(COL_KS), col(COL_VS), col(COL_KW), col(COL_VW), col(0),
            pl.BlockSpec((SEQ, HEAD_DIM), lambda b, g: (b, 0)),
            pl.BlockSpec((SEQ, HEAD_DIM), lambda b, g: (b, 0)),
            pl.BlockSpec((N_CMP_PAD, HEAD_DIM), lambda b, g: (0, 0)),
        ],
        out_specs=pl.BlockSpec((SEQ, gw), lambda b, g: (b, g)),
        out_shape=jax.ShapeDtypeStruct((TOKENS, NSA_HEADS * HEAD_DIM), BF16),
        scratch_shapes=[pltpu.VMEM((SEQ, 2 * HEAD_DIM), BF16), pltpu.VMEM((SEQ, 2 * HEAD_DIM), BF16),
                        pltpu.VMEM((SEQ, HEAD_DIM), BF16), pltpu.VMEM((SEQ, 2 * HEAD_DIM), BF16)],
        compiler_params=_params(("parallel", "parallel")),
        name="nsa_attention",
    )(proj, cmp_kv, cmp_kv, proj, proj, proj, proj, gates, rope_cs, rope_sn, jnp.asarray(_cmp_to_slc_weights()))


HGRN_HEADS_PER_STEP = 4


def _hgrn_chunk(qc, fc, vc, gc, lb, log_lb, log_1m, gn, tri, state_t):
    C, SUB = HGRN_CHUNK, HGRN_SUB
    row8 = lax.broadcasted_iota(jnp.int32, (8, HEAD_DIM), 0)
    qh = qc * jax.nn.sigmoid(qc)
    log_sig = jnp.minimum(fc, 0.0) - jnp.log(1.0 + jnp.exp(-jnp.abs(fc)))
    b = log_1m + log_sig
    lf = jnp.maximum(log_lb, b) + jnp.log(1.0 + jnp.exp(-jnp.abs(log_lb - b)))
    kh = (1.0 - lb) * jax.nn.sigmoid(-fc)
    a = jnp.dot(tri, lf, preferred_element_type=F32)
    inter = _dot_nt(qh * jnp.exp(a), state_t)
    parts = []
    for blk in range(C // SUB):
        lo = blk * SUB
        ab, qb, kb, vb = a[lo:lo + SUB], qh[lo:lo + SUB], kh[lo:lo + SUB], vc[lo:lo + SUB]
        acc = inter[lo:lo + SUB]
        if blk > 0:
            a_ref_row = a[lo - 1:lo]
            qt = qb * jnp.exp(ab - a_ref_row)
            kt = kh[0:lo] * jnp.exp(a_ref_row - a[0:lo])
            acc = acc + jnp.dot(_dot_nt(qt, kt), vc[0:lo], preferred_element_type=F32)
        for r0 in range(0, SUB, 8):
            ar, qr, acc_r = ab[r0:r0 + 8], qb[r0:r0 + 8], acc[r0:r0 + 8]
            for s in range(r0 + 8):
                rel = ar - ab[s:s + 1]
                if s >= r0:
                    rel = jnp.where(row8 >= s - r0, rel, -jnp.inf)
                w = jnp.sum(qr * kb[s:s + 1] * jnp.exp(rel), axis=-1, keepdims=True)
                acc_r = acc_r + w * vb[s:s + 1]
            parts.append(acc_r)
    o = jnp.concatenate(parts, axis=0)
    a_end = a[C - 1:C]
    kt_end = kh * jnp.exp(a_end - a)
    state_t = state_t * jnp.exp(a_end) + lax.dot_general(vc, kt_end, (((0,), (0,)), ((), ())),
                                                          preferred_element_type=F32)
    o = o * lax.rsqrt(jnp.mean(o * o, axis=-1, keepdims=True) + EPS) * gn
    return (o * jax.nn.sigmoid(gc)).astype(BF16), state_t


def _hgrn_kernel(q_ref, f_ref, i_ref, g_ref, lb_ref, gn_ref, tri_ref, o_ref):
    C, HP = HGRN_CHUNK, HGRN_HEADS_PER_STEP
    lbs = [lb_ref[hp] for hp in range(HP)]
    log_lbs = [jnp.log(lb) for lb in lbs]
    log_1ms = [jnp.log(1.0 - lb) for lb in lbs]
    gns = [gn_ref[hp] for hp in range(HP)]

    def chunk(c, states):
        r0 = pl.multiple_of(c * C, C)
        qc = q_ref[pl.ds(r0, C), :]
        fc = f_ref[pl.ds(r0, C), :]
        vc = i_ref[pl.ds(r0, C), :]
        gc = g_ref[pl.ds(r0, C), :]
        outs, new_states = [], []
        for hp in range(HP):
            cols = slice(hp * HEAD_DIM, (hp + 1) * HEAD_DIM)
            o, st = _hgrn_chunk(qc[:, cols], fc[:, cols], vc[:, cols], gc[:, cols], lbs[hp], log_lbs[hp], log_1ms[hp],
                                gns[hp], tri_ref[...], states[hp])
            outs.append(o)
            new_states.append(st)
        o_ref[pl.ds(r0, C), :] = jnp.concatenate(outs, axis=1)
        return tuple(new_states)

    lax.fori_loop(0, SEQ // C, chunk, tuple(jnp.zeros((HEAD_DIM, HEAD_DIM), F32) for _ in range(HP)), unroll=4)


def hgrn(proj, lb, gain):
    hp = HGRN_HEADS_PER_STEP
    col = lambda c: pl.BlockSpec((SEQ, hp * HEAD_DIM), lambda b, h: (b, c // hp + h))
    vec = pl.BlockSpec((hp, 1, HEAD_DIM), lambda b, h: (h, 0, 0))
    tri = np.tril(np.ones((HGRN_CHUNK, HGRN_CHUNK), np.float32))
    return pl.pallas_call(
        _hgrn_kernel,
        grid=(BATCH, HGRN_HEADS // hp),
        in_specs=[col(COL_HQ), col(COL_HF), col(COL_HI), col(COL_HG), vec, vec,
                  pl.BlockSpec((HGRN_CHUNK, HGRN_CHUNK), lambda b, h: (0, 0))],
        out_specs=pl.BlockSpec((SEQ, hp * HEAD_DIM), lambda b, h: (b, h)),
        out_shape=jax.ShapeDtypeStruct((TOKENS, HGRN_HEADS * HEAD_DIM), BF16),
        compiler_params=_params(("parallel", "parallel")),
        name="hgrn2",
    )(proj, proj, proj, proj, lb.reshape(HGRN_HEADS, 1, HEAD_DIM), gain.reshape(HGRN_HEADS, 1, HEAD_DIM),
      jnp.asarray(tri))


def _out_proj_kernel(on_ref, oh_ref, wn_ref, wh_ref, x_ref, g_ref, gain_ref, sh_ref, sc_ref, x1_ref, ht_ref):
    mix = (jnp.dot(on_ref[...], wn_ref[...], preferred_element_type=F32)
           + jnp.dot(oh_ref[...], wh_ref[...], preferred_element_type=F32))
    x1 = x_ref[...] + g_ref[0] * mix
    x1_ref[...] = x1
    ht_ref[...] = _rms_mod(x1, gain_ref[...], sc_ref[0], sh_ref[0]).T.astype(BF16)


def out_proj(o_nsa, o_hgrn, w_out, x, gain, mod, layer):
    tm = 256
    half = D_MODEL // 2
    rows = pl.BlockSpec((tm, half), lambda i: (i, 0))
    full = pl.BlockSpec((tm, D_MODEL), lambda i: (i, 0))
    return pl.pallas_call(
        _out_proj_kernel,
        grid=(TOKENS // tm,),
        in_specs=[rows, rows,
                  pl.BlockSpec((None, half, D_MODEL), lambda i: (layer, 0, 0)),
                  pl.BlockSpec((None, half, D_MODEL), lambda i: (layer, 1, 0)),
                  full, _mod_spec(layer, 2, SEQ // tm),
                  pl.BlockSpec((1, D_MODEL), lambda i: (0, 0)),
                  _mod_spec(layer, 3, SEQ // tm), _mod_spec(layer, 4, SEQ // tm)],
        out_specs=[full, pl.BlockSpec((D_MODEL, tm), lambda i: (0, i))],
        out_shape=[jax.ShapeDtypeStruct((TOKENS, D_MODEL), F32), jax.ShapeDtypeStruct((D_MODEL, TOKENS), BF16)],
        compiler_params=_params(("parallel",)),
        name="out_proj",
    )(o_nsa, o_hgrn, w_out, w_out, x, mod, gain.reshape(1, D_MODEL), mod, mod)


NOT_TOP = 127.0


def _pack_pairs(x):
    return pltpu.bitcast(x, jnp.uint32)


def _unpack_pairs(x):
    return pltpu.bitcast(x, BF16)


def _top_values(x, k, want_rank=False):
    vals = []
    rank = jnp.full(x.shape, NOT_TOP, F32) if want_rank else None
    for i in range(k):
        m = jnp.max(x, axis=0, keepdims=True)
        vals.append(m)
        hit = x == m
        if want_rank:
            rank = jnp.where(hit, float(i), rank)
        x = jnp.where(hit, -jnp.inf, x)
    return vals, rank


def _dot_tn(a, b):
    return lax.dot_general(a, b, (((0,), (0,)), ((), ())), preferred_element_type=F32)


def _peer_score_kernel(ht_ref, wq_ref, keys_ref, eb_ref, rb_ref, ea_ref, cnt_ref):
    k = PEER_TOPK
    qt = _dot_tn(wq_ref[...], ht_ref[...])
    for h in range(PEER_HEADS):
        sa = jnp.dot(keys_ref[2 * h], qt[(2 * h) * PEER_KEYS:(2 * h + 1) * PEER_KEYS], preferred_element_type=F32)
        sb = jnp.dot(keys_ref[2 * h + 1], qt[(2 * h + 1) * PEER_KEYS:(2 * h + 2) * PEER_KEYS],
                     preferred_element_type=F32)
        top_a, _ = _top_values(sa, k)
        top_b, rank_b = _top_values(sb, k, want_rank=True)
        a16 = jnp.concatenate(top_a, axis=0)
        b16 = jnp.concatenate(top_b, axis=0)
        cand = jnp.concatenate([a16 + top_b[0]] + [a16[0:8] + top_b[j] for j in range(1, 8)]
                               + [top_a[0] + b16[8:16]], axis=0)
        best, _ = _top_values(cand, k)
        z = jnp.zeros_like(best[0])
        for v in best:
            z = z + jnp.exp(v - best[0])
        cnt = jnp.zeros_like(sa)
        for j in range(k):
            cnt = cnt + jnp.where(sa + top_b[j] >= best[k - 1], 1.0, 0.0)
        half = slice(h * PEER_KEYS // 2, (h + 1) * PEER_KEYS // 2)
        eb_ref[half, :] = _pack_pairs(jnp.exp(sb - top_b[0]).astype(BF16))
        rb_ref[half, :] = _pack_pairs(rank_b.astype(BF16))
        ea_ref[h] = jnp.exp(sa - top_a[0]) / z
        cnt_ref[h] = cnt


def peer_scores(ht, wq, keys, layer):
    tm = 256
    rows = PEER_HEADS * PEER_KEYS
    spec2 = pl.BlockSpec((rows // 2, tm), lambda i: (0, i))
    shape2 = jax.ShapeDtypeStruct((rows // 2, TOKENS), jnp.uint32)
    spec3 = pl.BlockSpec((PEER_HEADS, PEER_KEYS, tm), lambda i: (0, 0, i))
    shape3 = jax.ShapeDtypeStruct((PEER_HEADS, PEER_KEYS, TOKENS), F32)
    return pl.pallas_call(
        _peer_score_kernel,
        grid=(TOKENS // tm,),
        in_specs=[pl.BlockSpec((D_MODEL, tm), lambda i: (0, i)),
                  pl.BlockSpec((None, D_MODEL, 2 * rows), lambda i: (layer, 0, 0)),
                  pl.BlockSpec((None, PEER_HEADS * 2, PEER_KEYS, PEER_KEYS), lambda i: (layer, 0, 0, 0))],
        out_specs=[spec2, spec2, spec3, spec3],
        out_shape=[shape2, shape2, shape3, shape3],
        compiler_params=_params(("parallel",)),
        name="peer_scores",
    )(ht, wq, keys)


PEER_TM = 1024
PEER_TE = 1024
PEER_LANES = 128


def _gelu_to_bf16(x):
    c = float(np.sqrt(2.0 / np.pi))
    t = jnp.tanh((x * (c + (c * 0.044715) * (x * x))).astype(BF16))
    return (0.5 * x.astype(BF16)) * (1.0 + t)


def _peer_dense_kernel(xt_ref, eb_in_ref, rb_in_ref, ea_ref, cnt_ref, u_ref, v_ref, o_ref, gw_ref, eb_ref, rb_ref):
    @pl.when(pl.program_id(1) == 0)
    def _():
        o_ref[...] = jnp.zeros_like(o_ref)
        eb_ref[...] = _unpack_pairs(eb_in_ref[...])
        rb_ref[...] = _unpack_pairs(rb_in_ref[...])

    def rows_bf16(ref, h, ab, cols):
        tile16 = jnp.broadcast_to(ref[h, ab:ab + 1, cols], (16, PEER_LANES)).astype(BF16)
        return jnp.concatenate([tile16] * (PEER_KEYS // 16), axis=0)

    act = jnp.dot(u_ref[...], xt_ref[...], preferred_element_type=F32)
    for ab in range(PEER_TE // PEER_KEYS):
        rows = slice(ab * PEER_KEYS, (ab + 1) * PEER_KEYS)
        for lt in range(PEER_TM // PEER_LANES):
            cols = slice(lt * PEER_LANES, (lt + 1) * PEER_LANES)
            w = jnp.zeros((PEER_KEYS, PEER_LANES), BF16)
            for h in range(PEER_HEADS):
                hb = slice(h * PEER_KEYS, (h + 1) * PEER_KEYS)
                cnt = rows_bf16(cnt_ref, h, ab, cols)
                w = w + jnp.where(rb_ref[hb, cols] < cnt, eb_ref[hb, cols] * rows_bf16(ea_ref, h, ab, cols), 0.0)
            gw = _gelu_to_bf16(act[rows, cols]) * w
            gw_ref[rows, cols] = gw
    o_ref[...] += _dot_tn(v_ref[...], gw_ref[...])


def peer_dense(ht, eb, rb, ea, cnt, u, v, layer):
    ab = PEER_TE // PEER_KEYS
    rows = pl.BlockSpec((PEER_HEADS, ab, PEER_TM), lambda i, j: (0, j, i))
    table = pl.BlockSpec((PEER_HEADS * PEER_KEYS // 2, PEER_TM), lambda i, j: (0, i))
    return pl.pallas_call(
        _peer_dense_kernel,
        grid=(TOKENS // PEER_TM, PEER_EXPERTS // PEER_TE),
        in_specs=[pl.BlockSpec((D_MODEL, PEER_TM), lambda i, j: (0, i)),
                  table, table, rows, rows,
                  pl.BlockSpec((None, PEER_TE, D_MODEL), lambda i, j: (layer, j, 0)),
                  pl.BlockSpec((None, PEER_TE, D_MODEL), lambda i, j: (layer, j, 0))],
        out_specs=pl.BlockSpec((D_MODEL, PEER_TM), lambda i, j: (0, i), pipeline_mode=pl.Buffered(1)),
        out_shape=jax.ShapeDtypeStruct((D_MODEL, TOKENS), F32),
        scratch_shapes=[pltpu.VMEM((PEER_TE, PEER_TM), BF16),
                        pltpu.VMEM((PEER_HEADS * PEER_KEYS, PEER_TM), BF16),
                        pltpu.VMEM((PEER_HEADS * PEER_KEYS, PEER_TM), BF16)],
        compiler_params=_params(("parallel", "arbitrary")),
        name="peer_dense",
    )(ht, eb, rb, ea, cnt, u, v)


def _peer_resid_kernel(x_ref, ot_ref, g_ref, gain_ref, sh_ref, sc_ref, x2_ref, h_ref):
    x2 = x_ref[...] + g_ref[0] * ot_ref[...].T
    x2_ref[...] = x2
    h_ref[...] = _rms_mod(x2, gain_ref[...], sc_ref[0], sh_ref[0]).astype(BF16)


def _final_kernel(x_ref, ot_ref, g_ref, gain_ref, o_ref):
    x2 = x_ref[...] + g_ref[0] * ot_ref[...].T
    o_ref[...] = x2 * lax.rsqrt(jnp.mean(x2 * x2, axis=-1, keepdims=True) + EPS) * gain_ref[...]


def peer_residual(x1, out_t, mod, layer, gain_next):
    tm = 256
    full = pl.BlockSpec((tm, D_MODEL), lambda i: (i, 0))
    return pl.pallas_call(
        _peer_resid_kernel,
        grid=(TOKENS // tm,),
        in_specs=[full, pl.BlockSpec((D_MODEL, tm), lambda i: (0, i)), _mod_spec(layer, 5, SEQ // tm),
                  pl.BlockSpec((1, D_MODEL), lambda i: (0, 0)),
                  _mod_spec(layer + 1, 0, SEQ // tm), _mod_spec(layer + 1, 1, SEQ // tm)],
        out_specs=[full, full],
        out_shape=[jax.ShapeDtypeStruct((TOKENS, D_MODEL), F32), jax.ShapeDtypeStruct((TOKENS, D_MODEL), BF16)],
        compiler_params=_params(("parallel",)),
        name="peer_residual",
    )(x1, out_t, mod, gain_next.reshape(1, D_MODEL), mod, mod)


def final_residual_norm(x1, out_t, mod, layer, gain):
    tm = 256
    full = pl.BlockSpec((tm, D_MODEL), lambda i: (i, 0))
    return pl.pallas_call(
        _final_kernel,
        grid=(TOKENS // tm,),
        in_specs=[full, pl.BlockSpec((D_MODEL, tm), lambda i: (0, i)), _mod_spec(layer, 5, SEQ // tm),
                  pl.BlockSpec((1, D_MODEL), lambda i: (0, 0))],
        out_specs=full,
        out_shape=jax.ShapeDtypeStruct((TOKENS, D_MODEL), F32),
        compiler_params=_params(("parallel",)),
        name="final_norm",
    )(x1, out_t, mod, gain.reshape(1, D_MODEL))


def _rope_tables(positions):
    inv = ROPE_THETA ** (-jnp.arange(0, 2 * ROT_HALF, 2, dtype=F32) / (2 * ROT_HALF))
    ang = positions.astype(F32).reshape(TOKENS, 1) * inv
    cos, sin = jnp.cos(ang), jnp.sin(ang)
    rest = HEAD_DIM - 2 * ROT_HALF
    cs = jnp.concatenate([cos, cos, jnp.ones((TOKENS, rest), F32)], axis=1)
    sn = jnp.concatenate([-sin, sin, jnp.zeros((TOKENS, rest), F32)], axis=1)
    return cs, sn


def kernel(x, c, positions, norm_mix, norm_ffn, w_ada, b_ada, w_in, w_out, cmp_k_pos, cmp_k_w1, cmp_k_b1, cmp_k_w2, cmp_k_b2, cmp_v_pos, cmp_v_w1, cmp_v_b1, cmp_v_w2, cmp_v_b2, hgrn_norm, hgrn_lb, peer_wq, peer_keys, peer_u, peer_v, final_norm):
    xf = x.reshape(TOKENS, D_MODEL)
    rope_cs, rope_sn = _rope_tables(positions)
    c8 = jnp.concatenate([c, jnp.zeros((8 - BATCH, D_MODEL), F32)], axis=0)
    mod = ada_mod(c8, w_ada, b_ada).reshape(DEPTH * 8 * 6, 1, D_MODEL)
    lb_all = jnp.cumsum(jax.nn.softmax(hgrn_lb.astype(F32), axis=0), axis=0)
    lb_all = lb_all - lb_all[0:1]

    w_nsa, w_hgrn, w_gate = split_w_in(w_in)
    w_out_b = w_out.astype(BF16)
    cmp_pos = jnp.stack([cmp_k_pos, cmp_v_pos], axis=1)
    cmp_w1 = jnp.stack([cmp_k_w1, cmp_v_w1], axis=1).astype(BF16)
    cmp_b1 = jnp.stack([cmp_k_b1, cmp_v_b1], axis=1).reshape(DEPTH, 2, 1, CMP_HIDDEN)
    cmp_w2 = jnp.stack([cmp_k_w2, cmp_v_w2], axis=1).astype(BF16)
    cmp_b2 = jnp.stack([cmp_k_b2, cmp_v_b2], axis=1).reshape(DEPTH, 2, 1, HEAD_DIM)
    wq_b = peer_wq.astype(BF16)
    keys = peer_keys.reshape(DEPTH, PEER_HEADS * 2, PEER_KEYS, PEER_KEYS)
    u_b = peer_u.astype(BF16)
    v_b = peer_v.astype(BF16)

    h = adaln(xf, norm_mix[0], mod, 0)
    for l in range(DEPTH):
        p_nsa = in_proj(h, w_nsa, l)
        p_hgrn = in_proj(h, w_hgrn, l)
        p_gate = in_proj(h, w_gate, l)
        cmp_kv = compress(p_nsa, cmp_pos, cmp_w1, cmp_b1, cmp_w2, cmp_b2, l)
        o_nsa = nsa_attention(p_nsa, p_gate, cmp_kv, rope_cs, rope_sn)
        o_hgrn = hgrn(p_hgrn, lb_all[l], hgrn_norm[l])
        x1, ht = out_proj(o_nsa, o_hgrn, w_out_b, xf, norm_ffn[l], mod, l)
        eb, rb, ea, cnt = peer_scores(ht, wq_b, keys, l)
        out_t = peer_dense(ht, eb, rb, ea, cnt, u_b, v_b, l)
        if l + 1 < DEPTH:
            xf, h = peer_residual(x1, out_t, mod, l, norm_mix[l + 1])
        else:
            xf = final_residual_norm(x1, out_t, mod, l, final_norm)
    return xf.reshape(BATCH, SEQ, D_MODEL)
```

```python
import numpy as np
import jax
import jax.numpy as jnp
from jax import lax
from jax.experimental import pallas as pl
from jax.experimental.pallas import tpu as pltpu

F32 = jnp.float32
BF16 = jnp.bfloat16

D_MODEL = 2048
BATCH = 4
SEQ = 2048
TOKENS = BATCH * SEQ
DEPTH = 2
EPS = 1e-6

HEAD_DIM = 128
NSA_HEADS = 8
NSA_GROUPS = 2
NSA_HPG = NSA_HEADS // NSA_GROUPS
CMP_BLOCK = 32
CMP_STRIDE = 16
CMP_HIDDEN = 256
N_CMP = (SEQ - CMP_BLOCK) // CMP_STRIDE + 1
N_CMP_PAD = 128
SLC_BLOCK = 64
N_SLC = SEQ // SLC_BLOCK
SLC_TOPN = 16
N_LOCAL_FORCED = 2
WINDOW = 512
ROT_HALF = 16
ROPE_THETA = 500000.0
Q_TILE = 128
WIN_KEYS = WINDOW + Q_TILE

HGRN_HEADS = 8
HGRN_CHUNK = 64
HGRN_SUB = 16

PEER_HEADS = 8
PEER_KEYS = 128
PEER_EXPERTS = PEER_KEYS * PEER_KEYS
PEER_TOPK = 16

NSA_COLS = 2560
COL_Q, COL_KC, COL_VC, COL_KS, COL_VS, COL_KW, COL_VW = 0, 8, 10, 12, 14, 16, 18
COL_HQ, COL_HF, COL_HI, COL_HG = 0, 8, 16, 24

VMEM_LIMIT = 61 * 1024 * 1024


def _params(sem):
    return pltpu.CompilerParams(dimension_semantics=sem, vmem_limit_bytes=VMEM_LIMIT)


def _ada_kernel(c_ref, w_ref, b_ref, o_ref):
    c = c_ref[...]
    o_ref[0] = jnp.dot(c * jax.nn.sigmoid(c), w_ref[0], preferred_element_type=F32) + b_ref[0]


def ada_mod(c8, w_ada, b_ada):
    tn = 1024
    n = 6 * D_MODEL
    return pl.pallas_call(
        _ada_kernel,
        grid=(DEPTH, n // tn),
        in_specs=[
            pl.BlockSpec((8, D_MODEL), lambda l, j: (0, 0)),
            pl.BlockSpec((1, D_MODEL, tn), lambda l, j: (l, 0, j)),
            pl.BlockSpec((1, 1, tn), lambda l, j: (l, 0, j)),
        ],
        out_specs=pl.BlockSpec((1, 8, tn), lambda l, j: (l, 0, j)),
        out_shape=jax.ShapeDtypeStruct((DEPTH, 8, n), F32),
        compiler_params=_params(("parallel", "parallel")),
        name="ada_mod",
    )(c8, w_ada, b_ada.reshape(DEPTH, 1, n))


def _mod_spec(layer, k, rows_per_batch):
    return pl.BlockSpec((1, 1, D_MODEL), lambda i: ((layer * 8 + i // rows_per_batch) * 6 + k, 0, 0))


def _rms_mod(x, gain, scale, shift):
    y = x * lax.rsqrt(jnp.mean(x * x, axis=-1, keepdims=True) + EPS) * gain
    return y * (1.0 + scale) + shift


def _adaln_kernel(x_ref, gain_ref, sh_ref, sc_ref, h_ref):
    h_ref[...] = _rms_mod(x_ref[...], gain_ref[...], sc_ref[0], sh_ref[0]).astype(BF16)


def adaln(x, gain, mod, layer):
    tm = 256
    return pl.pallas_call(
        _adaln_kernel,
        grid=(TOKENS // tm,),
        in_specs=[
            pl.BlockSpec((tm, D_MODEL), lambda i: (i, 0)),
            pl.BlockSpec((1, D_MODEL), lambda i: (0, 0)),
            _mod_spec(layer, 0, SEQ // tm),
            _mod_spec(layer, 1, SEQ // tm),
        ],
        out_specs=pl.BlockSpec((tm, D_MODEL), lambda i: (i, 0)),
        out_shape=jax.ShapeDtypeStruct((TOKENS, D_MODEL), BF16),
        compiler_params=_params(("parallel",)),
        name="adaln",
    )(x, gain.reshape(1, D_MODEL), mod, mod)


def _matmul_nt_kernel(a_ref, bt_ref, o_ref):
    o_ref[...] = lax.dot_general(a_ref[...], bt_ref[...], (((1,), (1,)), ((), ())), preferred_element_type=F32)


def in_proj(h, wt, layer):
    n = wt.shape[1]
    tm = 1024
    tn = next(t for t in (1280, 1024, 256) if n % t == 0)
    return pl.pallas_call(
        _matmul_nt_kernel,
        grid=(TOKENS // tm, n // tn),
        in_specs=[
            pl.BlockSpec((tm, D_MODEL), lambda i, j: (i, 0)),
            pl.BlockSpec((None, tn, D_MODEL), lambda i, j: (layer, j, 0)),
        ],
        out_specs=pl.BlockSpec((tm, tn), lambda i, j: (i, j)),
        out_shape=jax.ShapeDtypeStruct((TOKENS, n), F32),
        compiler_params=_params(("parallel", "parallel")),
        name="in_proj",
    )(h, wt)


def split_w_in(w):
    wt = jnp.swapaxes(w, 1, 2)
    gl = wt[:, NSA_COLS:NSA_COLS + 24]
    z = jnp.zeros((w.shape[0], 116, D_MODEL), w.dtype)
    gates = jnp.concatenate([gl[:, :12], z, gl[:, 12:], z], axis=1)
    return wt[:, :NSA_COLS].astype(BF16), wt[:, NSA_COLS + 24:].astype(BF16), gates.astype(BF16)


def _compress_kernel(t_ref, pos_ref, w1_ref, b1_ref, w2_ref, b2_ref, o_ref, pad_ref):
    pad_ref[0:SEQ, :] = t_ref[...]
    pad_ref[SEQ:SEQ + CMP_BLOCK, :] = jnp.zeros((CMP_BLOCK, HEAD_DIM), F32)
    acc = jnp.zeros((N_CMP_PAD, CMP_HIDDEN), F32)
    for l in range(CMP_BLOCK):
        tl = pad_ref[pl.ds(l, N_CMP_PAD, stride=CMP_STRIDE), :] + pos_ref[0, l:l + 1, :]
        acc = acc + jnp.dot(tl.astype(BF16), w1_ref[0, l * HEAD_DIM:(l + 1) * HEAD_DIM, :],
                            preferred_element_type=F32)
    hid = jax.nn.gelu(acc + b1_ref[0])
    o_ref[0, 0] = jnp.dot(hid.astype(BF16), w2_ref[0], preferred_element_type=F32) + b2_ref[0]


def compress(proj, pos, w1, b1, w2, b2, layer):
    return pl.pallas_call(
        _compress_kernel,
        grid=(2, BATCH, NSA_GROUPS),
        in_specs=[
            pl.BlockSpec((SEQ, HEAD_DIM), lambda kv, b, g: (b, COL_KC + 2 * kv + g)),
            pl.BlockSpec((None, 1, CMP_BLOCK, HEAD_DIM), lambda kv, b, g: (layer, kv, 0, 0)),
            pl.BlockSpec((None, 1, CMP_BLOCK * HEAD_DIM, CMP_HIDDEN), lambda kv, b, g: (layer, kv, 0, 0)),
            pl.BlockSpec((None, 1, 1, CMP_HIDDEN), lambda kv, b, g: (layer, kv, 0, 0)),
            pl.BlockSpec((None, 1, CMP_HIDDEN, HEAD_DIM), lambda kv, b, g: (layer, kv, 0, 0)),
            pl.BlockSpec((None, 1, 1, HEAD_DIM), lambda kv, b, g: (layer, kv, 0, 0)),
        ],
        out_specs=pl.BlockSpec((1, 1, N_CMP_PAD, HEAD_DIM), lambda kv, b, g: (kv, b * NSA_GROUPS + g, 0, 0)),
        out_shape=jax.ShapeDtypeStruct((2, BATCH * NSA_GROUPS, N_CMP_PAD, HEAD_DIM), F32),
        scratch_shapes=[pltpu.VMEM((SEQ + CMP_BLOCK, HEAD_DIM), F32)],
        compiler_params=_params(("arbitrary", "arbitrary", "arbitrary")),
        name="nsa_compress",
    )(proj, pos, w1, b1, w2, b2)


def _rope(t, cs, sn):
    lane = lax.broadcasted_iota(jnp.int32, t.shape, 1)
    swapped = jnp.where(lane < ROT_HALF, pltpu.roll(t, HEAD_DIM - ROT_HALF, axis=1), pltpu.roll(t, ROT_HALF, axis=1))
    return t * cs + swapped * sn


def _dot_nt(a, b):
    return lax.dot_general(a, b, (((1,), (1,)), ((), ())), preferred_element_type=F32)


MASKED = -1e30
GROUP_TILES = 2


def _nsa_kernel(q_ref, kc_ref, vc_ref, ks_ref, vs_ref, kw_ref, vw_ref, gl_ref, cs_ref, sn_ref, wov_ref,
                o_ref, kaug, vaug, kwr, vwaug):
    scale = HEAD_DIM ** -0.5
    neg_inf = -jnp.inf
    row = lax.broadcasted_iota(jnp.int32, (SEQ, HEAD_DIM), 0)
    lane = lax.broadcasted_iota(jnp.int32, (SEQ, HEAD_DIM), 1)
    kaug[:, 0:HEAD_DIM] = _rope(ks_ref[...], cs_ref[...], sn_ref[...]).astype(BF16)
    kaug[:, HEAD_DIM:2 * HEAD_DIM] = jnp.where(lane == (row >> 6), 1.0, 0.0).astype(BF16)
    kwr[...] = _rope(kw_ref[...], cs_ref[...], sn_ref[...]).astype(BF16)
    ones = jnp.ones((SEQ, HEAD_DIM), BF16)
    vaug[:, 0:HEAD_DIM] = vs_ref[...].astype(BF16)
    vaug[:, HEAD_DIM:2 * HEAD_DIM] = ones
    vwaug[:, 0:HEAD_DIM] = vw_ref[...].astype(BF16)
    vwaug[:, HEAD_DIM:2 * HEAD_DIM] = ones
    kcb = kc_ref[0, 0].astype(BF16)
    vcb = vc_ref[0, 0].astype(BF16)

    def tile(i, extent):
        t0 = pl.multiple_of(i * Q_TILE, Q_TILE)
        q4 = q_ref[pl.ds(t0, Q_TILE), :] * scale
        cs = cs_ref[pl.ds(t0, Q_TILE), :]
        sn = sn_ref[pl.ds(t0, Q_TILE), :]
        gate = jax.nn.sigmoid(gl_ref[pl.ds(t0, Q_TILE), :])
        qh = [q4[:, h * HEAD_DIM:(h + 1) * HEAD_DIM] for h in range(NSA_HPG)]

        qs = jnp.concatenate(qh, axis=0).astype(BF16)
        sc = _dot_nt(qs, kcb)
        rows = lax.broadcasted_iota(jnp.int32, sc.shape, 0)
        ncol = lax.broadcasted_iota(jnp.int32, sc.shape, 1)
        tok = t0 + (rows & (Q_TILE - 1))
        valid = ncol * CMP_STRIDE + (CMP_BLOCK - 1) <= tok
        s = jnp.where(valid, sc, neg_inf)
        m = jnp.max(s, axis=-1, keepdims=True)
        m = jnp.where(m == neg_inf, 0.0, m)
        e = jnp.where(valid, jnp.exp(s - m), 0.0)
        p = e / jnp.maximum(jnp.sum(e, axis=-1, keepdims=True), 1e-30)
        o_cmp = jnp.dot(p.astype(BF16), vcb, preferred_element_type=F32)

        p4 = p[0:Q_TILE] + p[Q_TILE:2 * Q_TILE] + p[2 * Q_TILE:3 * Q_TILE] + p[3 * Q_TILE:4 * Q_TILE]
        imp = jnp.dot(p4, wov_ref[...], preferred_element_type=F32)
        jj = lax.broadcasted_iota(jnp.int32, imp.shape, 1)
        tt = t0 + lax.broadcasted_iota(jnp.int32, imp.shape, 0)
        back = (tt >> 6) - jj
        forced = (jj == 0) | (back.astype(jnp.uint32) < N_LOCAL_FORCED)
        score = jnp.where(forced, jnp.inf, jnp.where(jj * SLC_BLOCK <= tt, imp, neg_inf))
        score_t = score.T[0:N_SLC]
        jrow = lax.broadcasted_iota(jnp.int32, score_t.shape, 0)
        rank = jnp.zeros(score_t.shape, F32)
        for jp in range(N_SLC):
            other = score_t[jp:jp + 1]
            later = jnp.where(jrow > jp, 1.0, 0.0)
            rank = rank + jnp.where(other > score_t, 1.0, jnp.where(other == score_t, later, 0.0))
        bias_t = jnp.where(rank < SLC_TOPN, 0.0, MASKED)
        sel_bias = jnp.concatenate([bias_t, jnp.zeros((HEAD_DIM - N_SLC, Q_TILE), F32)], axis=0).T

        tail = GROUP_TILES * Q_TILE
        rows4 = NSA_HPG * Q_TILE
        kpos = (extent - tail) + lax.broadcasted_iota(jnp.int32, (rows4, tail), 1)
        ok_tail = kpos <= t0 + (lax.broadcasted_iota(jnp.int32, (rows4, tail), 0) & (Q_TILE - 1))

        qr = jnp.concatenate([_rope(qh[h], cs, sn) for h in range(NSA_HPG)], axis=0)
        qa = jnp.concatenate([qr, jnp.concatenate([sel_bias] * NSA_HPG, axis=0)], axis=1).astype(BF16)
        s1 = _dot_nt(qa, kaug[0:extent, :])
        s1_tail = jnp.where(ok_tail, s1[:, extent - tail:], MASKED)
        s1 = s1_tail if extent == tail else jnp.concatenate([s1[:, :extent - tail], s1_tail], axis=1)
        e1 = jnp.exp(s1 - jnp.max(s1, axis=-1, keepdims=True)).astype(BF16)
        r1 = jnp.dot(e1, vaug[0:extent, :], preferred_element_type=F32)
        o_slc = r1[:, :HEAD_DIM] / r1[:, HEAD_DIM:HEAD_DIM + 1]

        ws = pl.multiple_of(jnp.clip(t0 - WINDOW, 0, SEQ - WIN_KEYS), Q_TILE)
        wk = ws + lax.broadcasted_iota(jnp.int32, (rows4, WIN_KEYS), 1)
        wt = t0 + (lax.broadcasted_iota(jnp.int32, (rows4, WIN_KEYS), 0) & (Q_TILE - 1))
        ok_win = (wt - wk).astype(jnp.uint32) < WINDOW
        s2 = jnp.where(ok_win, _dot_nt(qr.astype(BF16), kwr[pl.ds(ws, WIN_KEYS), :]), MASKED)
        e2 = jnp.exp(s2 - jnp.max(s2, axis=-1, keepdims=True)).astype(BF16)
        r2 = jnp.dot(e2, vwaug[pl.ds(ws, WIN_KEYS), :], preferred_element_type=F32)
        o_win = r2[:, :HEAD_DIM] / r2[:, HEAD_DIM:HEAD_DIM + 1]

        outs = []
        for h in range(NSA_HPG):
            hr = slice(h * Q_TILE, (h + 1) * Q_TILE)
            outs.append(gate[:, 3 * h:3 * h + 1] * o_cmp[hr] + gate[:, 3 * h + 1:3 * h + 2] * o_slc[hr]
                        + gate[:, 3 * h + 2:3 * h + 3] * o_win[hr])
        o_ref[pl.ds(t0, Q_TILE), :] = jnp.concatenate(outs, axis=1).astype(BF16)

    for grp in range(SEQ // (GROUP_TILES * Q_TILE)):
        extent = (grp + 1) * GROUP_TILES * Q_TILE

        def body(ii, carry, grp=grp, extent=extent):
            tile(grp * GROUP_TILES + ii, extent)
            return carry

        lax.fori_loop(0, GROUP_TILES, body, 0)


def _cmp_to_slc_weights():
    cs = np.arange(N_CMP) * CMP_STRIDE
    ce = cs + CMP_BLOCK
    ss = np.arange(N_SLC) * SLC_BLOCK
    se = ss + SLC_BLOCK
    ov = np.clip(np.minimum(ce[:, None], se[None, :]) - np.maximum(cs[:, None], ss[None, :]), 0, None)
    w = np.zeros((N_CMP_PAD, HEAD_DIM), np.float32)
    w[:N_CMP, :N_SLC] = ov / CMP_BLOCK
    return w


def nsa_attention(proj, gates, cmp_kv, rope_cs, rope_sn):
    gw = NSA_HPG * HEAD_DIM
    col = lambda c: pl.BlockSpec((SEQ, HEAD_DIM), lambda b, g: (b, c + g))
    return pl.pallas_call(
        _nsa_kernel,
        grid=(BATCH, NSA_GROUPS),
        in_specs=[
            pl.BlockSpec((SEQ, gw), lambda b, g: (b, g)),
            pl.BlockSpec((1, 1, N_CMP_PAD, HEAD_DIM), lambda b, g: (0, b * NSA_GROUPS + g, 0, 0)),
            pl.BlockSpec((1, 1, N_CMP_PAD, HEAD_DIM), lambda b, g: (1, b * NSA_GROUPS + g, 0, 0)),
            col(COL_KS), col(COL_VS), col(COL_KW), col(COL_VW), col(0),
            pl.BlockSpec((SEQ, HEAD_DIM), lambda b, g: (b, 0)),
            pl.BlockSpec((SEQ, HEAD_DIM), lambda b, g: (b, 0)),
            pl.BlockSpec((N_CMP_PAD, HEAD_DIM), lambda b, g: (0, 0)),
        ],
        out_specs=pl.BlockSpec((SEQ, gw), lambda b, g: (b, g)),
        out_shape=jax.ShapeDtypeStruct((TOKENS, NSA_HEADS * HEAD_DIM), BF16),
        scratch_shapes=[pltpu.VMEM((SEQ, 2 * HEAD_DIM), BF16), pltpu.VMEM((SEQ, 2 * HEAD_DIM), BF16),
                        pltpu.VMEM((SEQ, HEAD_DIM), BF16), pltpu.VMEM((SEQ, 2 * HEAD_DIM), BF16)],
        compiler_params=_params(("parallel", "parallel")),
        name="nsa_attention",
    )(proj, cmp_kv, cmp_kv, proj, proj, proj, proj, gates, rope_cs, rope_sn, jnp.asarray(_cmp_to_slc_weights()))


HGRN_HEADS_PER_STEP = 4


def _hgrn_chunk(qc, fc, vc, gc, lb, log_lb, log_1m, gn, tri, state_t):
    C, SUB = HGRN_CHUNK, HGRN_SUB
    row8 = lax.broadcasted_iota(jnp.int32, (8, HEAD_DIM), 0)
    qh = qc * jax.nn.sigmoid(qc)
    log_sig = jnp.minimum(fc, 0.0) - jnp.log(1.0 + jnp.exp(-jnp.abs(fc)))
    b = log_1m + log_sig
    lf = jnp.maximum(log_lb, b) + jnp.log(1.0 + jnp.exp(-jnp.abs(log_lb - b)))
    kh = (1.0 - lb) * jax.nn.sigmoid(-fc)
    a = jnp.dot(tri, lf, preferred_element_type=F32)
    inter = _dot_nt(qh * jnp.exp(a), state_t)
    parts = []
    for blk in range(C // SUB):
        lo = blk * SUB
        ab, qb, kb, vb = a[lo:lo + SUB], qh[lo:lo + SUB], kh[lo:lo + SUB], vc[lo:lo + SUB]
        acc = inter[lo:lo + SUB]
        if blk > 0:
            a_ref_row = a[lo - 1:lo]
            qt = qb * jnp.exp(ab - a_ref_row)
            kt = kh[0:lo] * jnp.exp(a_ref_row - a[0:lo])
            acc = acc + jnp.dot(_dot_nt(qt, kt), vc[0:lo], preferred_element_type=F32)
        for r0 in range(0, SUB, 8):
            ar, qr, acc_r = ab[r0:r0 + 8], qb[r0:r0 + 8], acc[r0:r0 + 8]
            for s in range(r0 + 8):
                rel = ar - ab[s:s + 1]
                if s >= r0:
                    rel = jnp.where(row8 >= s - r0, rel, -jnp.inf)
                w = jnp.sum(qr * kb[s:s + 1] * jnp.exp(rel), axis=-1, keepdims=True)
                acc_r = acc_r + w * vb[s:s + 1]
            parts.append(acc_r)
    o = jnp.concatenate(parts, axis=0)
    a_end = a[C - 1:C]
    kt_end = kh * jnp.exp(a_end - a)
    state_t = state_t * jnp.exp(a_end) + lax.dot_general(vc, kt_end, (((0,), (0,)), ((), ())),
                                                          preferred_element_type=F32)
    o = o * lax.rsqrt(jnp.mean(o * o, axis=-1, keepdims=True) + EPS) * gn
    return (o * jax.nn.sigmoid(gc)).astype(BF16), state_t


def _hgrn_kernel(q_ref, f_ref, i_ref, g_ref, lb_ref, gn_ref, tri_ref, o_ref):
    C, HP = HGRN_CHUNK, HGRN_HEADS_PER_STEP
    lbs = [lb_ref[hp] for hp in range(HP)]
    log_lbs = [jnp.log(lb) for lb in lbs]
    log_1ms = [jnp.log(1.0 - lb) for lb in lbs]
    gns = [gn_ref[hp] for hp in range(HP)]

    def chunk(c, states):
        r0 = pl.multiple_of(c * C, C)
        qc = q_ref[pl.ds(r0, C), :]
        fc = f_ref[pl.ds(r0, C), :]
        vc = i_ref[pl.ds(r0, C), :]
        gc = g_ref[pl.ds(r0, C), :]
        outs, new_states = [], []
        for hp in range(HP):
            cols = slice(hp * HEAD_DIM, (hp + 1) * HEAD_DIM)
            o, st = _hgrn_chunk(qc[:, cols], fc[:, cols], vc[:, cols], gc[:, cols], lbs[hp], log_lbs[hp], log_1ms[hp],
                                gns[hp], tri_ref[...], states[hp])
            outs.append(o)
            new_states.append(st)
        o_ref[pl.ds(r0, C), :] = jnp.concatenate(outs, axis=1)
        return tuple(new_states)

    lax.fori_loop(0, SEQ // C, chunk, tuple(jnp.zeros((HEAD_DIM, HEAD_DIM), F32) for _ in range(HP)), unroll=4)


def hgrn(proj, lb, gain):
    hp = HGRN_HEADS_PER_STEP
    col = lambda c: pl.BlockSpec((SEQ, hp * HEAD_DIM), lambda b, h: (b, c // hp + h))
    vec = pl.BlockSpec((hp, 1, HEAD_DIM), lambda b, h: (h, 0, 0))
    tri = np.tril(np.ones((HGRN_CHUNK, HGRN_CHUNK), np.float32))
    return pl.pallas_call(
        _hgrn_kernel,
        grid=(BATCH, HGRN_HEADS // hp),
        in_specs=[col(COL_HQ), col(COL_HF), col(COL_HI), col(COL_HG), vec, vec,
                  pl.BlockSpec((HGRN_CHUNK, HGRN_CHUNK), lambda b, h: (0, 0))],
        out_specs=pl.BlockSpec((SEQ, hp * HEAD_DIM), lambda b, h: (b, h)),
        out_shape=jax.ShapeDtypeStruct((TOKENS, HGRN_HEADS * HEAD_DIM), BF16),
        compiler_params=_params(("parallel", "parallel")),
        name="hgrn2",
    )(proj, proj, proj, proj, lb.reshape(HGRN_HEADS, 1, HEAD_DIM), gain.reshape(HGRN_HEADS, 1, HEAD_DIM),
      jnp.asarray(tri))


def _out_proj_kernel(on_ref, oh_ref, wn_ref, wh_ref, x_ref, g_ref, gain_ref, sh_ref, sc_ref, x1_ref, ht_ref):
    mix = (jnp.dot(on_ref[...], wn_ref[...], preferred_element_type=F32)
           + jnp.dot(oh_ref[...], wh_ref[...], preferred_element_type=F32))
    x1 = x_ref[...] + g_ref[0] * mix
    x1_ref[...] = x1
    ht_ref[...] = _rms_mod(x1, gain_ref[...], sc_ref[0], sh_ref[0]).T.astype(BF16)


def out_proj(o_nsa, o_hgrn, w_out, x, gain, mod, layer):
    tm = 256
    half = D_MODEL // 2
    rows = pl.BlockSpec((tm, half), lambda i: (i, 0))
    full = pl.BlockSpec((tm, D_MODEL), lambda i: (i, 0))
    return pl.pallas_call(
        _out_proj_kernel,
        grid=(TOKENS // tm,),
        in_specs=[rows, rows,
                  pl.BlockSpec((None, half, D_MODEL), lambda i: (layer, 0, 0)),
                  pl.BlockSpec((None, half, D_MODEL), lambda i: (layer, 1, 0)),
                  full, _mod_spec(layer, 2, SEQ // tm),
                  pl.BlockSpec((1, D_MODEL), lambda i: (0, 0)),
                  _mod_spec(layer, 3, SEQ // tm), _mod_spec(layer, 4, SEQ // tm)],
        out_specs=[full, pl.BlockSpec((D_MODEL, tm), lambda i: (0, i))],
        out_shape=[jax.ShapeDtypeStruct((TOKENS, D_MODEL), F32), jax.ShapeDtypeStruct((D_MODEL, TOKENS), BF16)],
        compiler_params=_params(("parallel",)),
        name="out_proj",
    )(o_nsa, o_hgrn, w_out, w_out, x, mod, gain.reshape(1, D_MODEL), mod, mod)


NOT_TOP = 127.0


def _pack_pairs(x):
    return pltpu.bitcast(x, jnp.uint32)


def _unpack_pairs(x):
    return pltpu.bitcast(x, BF16)


def _top_values(x, k, want_rank=False):
    vals = []
    rank = jnp.full(x.shape, NOT_TOP, F32) if want_rank else None
    for i in range(k):
        m = jnp.max(x, axis=0, keepdims=True)
        vals.append(m)
        hit = x == m
        if want_rank:
            rank = jnp.where(hit, float(i), rank)
        x = jnp.where(hit, -jnp.inf, x)
    return vals, rank


def _dot_tn(a, b):
    return lax.dot_general(a, b, (((0,), (0,)), ((), ())), preferred_element_type=F32)


def _peer_score_kernel(ht_ref, wq_ref, keys_ref, eb_ref, rb_ref, ea_ref, cnt_ref):
    k = PEER_TOPK
    qt = _dot_tn(wq_ref[...], ht_ref[...])
    for h in range(PEER_HEADS):
        sa = jnp.dot(keys_ref[2 * h], qt[(2 * h) * PEER_KEYS:(2 * h + 1) * PEER_KEYS], preferred_element_type=F32)
        sb = jnp.dot(keys_ref[2 * h + 1], qt[(2 * h + 1) * PEER_KEYS:(2 * h + 2) * PEER_KEYS],
                     preferred_element_type=F32)
        top_a, _ = _top_values(sa, k)
        top_b, rank_b = _top_values(sb, k, want_rank=True)
        a16 = jnp.concatenate(top_a, axis=0)
        b16 = jnp.concatenate(top_b, axis=0)
        cand = jnp.concatenate([a16 + top_b[0]] + [a16[0:8] + top_b[j] for j in range(1, 8)]
                               + [top_a[0] + b16[8:16]], axis=0)
        best, _ = _top_values(cand, k)
        z = jnp.zeros_like(best[0])
        for v in best:
            z = z + jnp.exp(v - best[0])
        cnt = jnp.zeros_like(sa)
        for j in range(k // 2):
            cnt = cnt + jnp.where(sa + top_b[j] >= best[k - 1], 1.0, 0.0)
        extra = jnp.zeros_like(best[0])
        for j in range(k // 2, k):
            extra = extra + jnp.where(top_a[0] + top_b[j] >= best[k - 1], 1.0, 0.0)
        cnt = cnt + jnp.where(sa == top_a[0], extra, 0.0)
        half = slice(h * PEER_KEYS // 2, (h + 1) * PEER_KEYS // 2)
        eb_ref[half, :] = _pack_pairs(jnp.exp(sb - top_b[0]).astype(BF16))
        rb_ref[half, :] = _pack_pairs(rank_b.astype(BF16))
        ea_ref[h] = jnp.exp(sa - top_a[0]) / z
        cnt_ref[h] = cnt


def peer_scores(ht, wq, keys, layer):
    tm = 256
    rows = PEER_HEADS * PEER_KEYS
    spec2 = pl.BlockSpec((rows // 2, tm), lambda i: (0, i))
    shape2 = jax.ShapeDtypeStruct((rows // 2, TOKENS), jnp.uint32)
    spec3 = pl.BlockSpec((PEER_HEADS, PEER_KEYS, tm), lambda i: (0, 0, i))
    shape3 = jax.ShapeDtypeStruct((PEER_HEADS, PEER_KEYS, TOKENS), F32)
    return pl.pallas_call(
        _peer_score_kernel,
        grid=(TOKENS // tm,),
        in_specs=[pl.BlockSpec((D_MODEL, tm), lambda i: (0, i)),
                  pl.BlockSpec((None, D_MODEL, 2 * rows), lambda i: (layer, 0, 0)),
                  pl.BlockSpec((None, PEER_HEADS * 2, PEER_KEYS, PEER_KEYS), lambda i: (layer, 0, 0, 0))],
        out_specs=[spec2, spec2, spec3, spec3],
        out_shape=[shape2, shape2, shape3, shape3],
        compiler_params=_params(("parallel",)),
        name="peer_scores",
    )(ht, wq, keys)


PEER_TM = 1024
PEER_TE = 1024
PEER_LANES = 128


def _gelu_to_bf16(x):
    c = float(np.sqrt(2.0 / np.pi))
    t = jnp.tanh((x * (c + (c * 0.044715) * (x * x))).astype(BF16))
    return (0.5 * x.astype(BF16)) * (1.0 + t)


def _peer_dense_kernel(xt_ref, eb_ref, rb_ref, ea_ref, cnt_ref, u_ref, v_ref, o_ref, gw_ref):
    @pl.when(pl.program_id(1) == 0)
    def _():
        o_ref[...] = jnp.zeros_like(o_ref)

    def rows_bf16(ref, h, ab, cols):
        tile16 = jnp.broadcast_to(ref[h, ab:ab + 1, cols], (16, PEER_LANES)).astype(BF16)
        return jnp.concatenate([tile16] * (PEER_KEYS // 16), axis=0)

    act = jnp.dot(u_ref[...], xt_ref[...], preferred_element_type=F32)
    for ab in range(PEER_TE // PEER_KEYS):
        rows = slice(ab * PEER_KEYS, (ab + 1) * PEER_KEYS)
        for lt in range(PEER_TM // PEER_LANES):
            cols = slice(lt * PEER_LANES, (lt + 1) * PEER_LANES)
            w = jnp.zeros((PEER_KEYS, PEER_LANES), BF16)
            for h in range(PEER_HEADS):
                hb = slice(h * PEER_KEYS // 2, (h + 1) * PEER_KEYS // 2)
                eb = _unpack_pairs(eb_ref[hb, cols])
                rb = _unpack_pairs(rb_ref[hb, cols])
                cnt = rows_bf16(cnt_ref, h, ab, cols)
                w = w + jnp.where(rb < cnt, eb * rows_bf16(ea_ref, h, ab, cols), 0.0)
            gw = _gelu_to_bf16(act[rows, cols]) * w
            gw_ref[rows, cols] = gw
    o_ref[...] += _dot_tn(v_ref[...], gw_ref[...])


def peer_dense(ht, eb, rb, ea, cnt, u, v, layer):
    ab = PEER_TE // PEER_KEYS
    rows = pl.BlockSpec((PEER_HEADS, ab, PEER_TM), lambda i, j: (0, j, i))
    table = pl.BlockSpec((PEER_HEADS * PEER_KEYS // 2, PEER_TM), lambda i, j: (0, i))
    return pl.pallas_call(
        _peer_dense_kernel,
        grid=(TOKENS // PEER_TM, PEER_EXPERTS // PEER_TE),
        in_specs=[pl.BlockSpec((D_MODEL, PEER_TM), lambda i, j: (0, i)),
                  table, table, rows, rows,
                  pl.BlockSpec((None, PEER_TE, D_MODEL), lambda i, j: (layer, j, 0)),
                  pl.BlockSpec((None, PEER_TE, D_MODEL), lambda i, j: (layer, j, 0))],
        out_specs=pl.BlockSpec((D_MODEL, PEER_TM), lambda i, j: (0, i)),
        out_shape=jax.ShapeDtypeStruct((D_MODEL, TOKENS), F32),
        scratch_shapes=[pltpu.VMEM((PEER_TE, PEER_TM), BF16)],
        compiler_params=_params(("parallel", "arbitrary")),
        name="peer_dense",
    )(ht, eb, rb, ea, cnt, u, v)


def _peer_resid_kernel(x_ref, ot_ref, g_ref, gain_ref, sh_ref, sc_ref, x2_ref, h_ref):
    x2 = x_ref[...] + g_ref[0] * ot_ref[...].T
    x2_ref[...] = x2
    h_ref[...] = _rms_mod(x2, gain_ref[...], sc_ref[0], sh_ref[0]).astype(BF16)


def _final_kernel(x_ref, ot_ref, g_ref, gain_ref, o_ref):
    x2 = x_ref[...] + g_ref[0] * ot_ref[...].T
    o_ref[...] = x2 * lax.rsqrt(jnp.mean(x2 * x2, axis=-1, keepdims=True) + EPS) * gain_ref[...]


def peer_residual(x1, out_t, mod, layer, gain_next):
    tm = 256
    full = pl.BlockSpec((tm, D_MODEL), lambda i: (i, 0))
    return pl.pallas_call(
        _peer_resid_kernel,
        grid=(TOKENS // tm,),
        in_specs=[full, pl.BlockSpec((D_MODEL, tm), lambda i: (0, i)), _mod_spec(layer, 5, SEQ // tm),
                  pl.BlockSpec((1, D_MODEL), lambda i: (0, 0)),
                  _mod_spec(layer + 1, 0, SEQ // tm), _mod_spec(layer + 1, 1, SEQ // tm)],
        out_specs=[full, full],
        out_shape=[jax.ShapeDtypeStruct((TOKENS, D_MODEL), F32), jax.ShapeDtypeStruct((TOKENS, D_MODEL), BF16)],
        compiler_params=_params(("parallel",)),
        name="peer_residual",
    )(x1, out_t, mod, gain_next.reshape(1, D_MODEL), mod, mod)


def final_residual_norm(x1, out_t, mod, layer, gain):
    tm = 256
    full = pl.BlockSpec((tm, D_MODEL), lambda i: (i, 0))
    return pl.pallas_call(
        _final_kernel,
        grid=(TOKENS // tm,),
        in_specs=[full, pl.BlockSpec((D_MODEL, tm), lambda i: (0, i)), _mod_spec(layer, 5, SEQ // tm),
                  pl.BlockSpec((1, D_MODEL), lambda i: (0, 0))],
        out_specs=full,
        out_shape=jax.ShapeDtypeStruct((TOKENS, D_MODEL), F32),
        compiler_params=_params(("parallel",)),
        name="final_norm",
    )(x1, out_t, mod, gain.reshape(1, D_MODEL))


def _rope_tables(positions):
    inv = ROPE_THETA ** (-jnp.arange(0, 2 * ROT_HALF, 2, dtype=F32) / (2 * ROT_HALF))
    ang = positions.astype(F32).reshape(TOKENS, 1) * inv
    cos, sin = jnp.cos(ang), jnp.sin(ang)
    rest = HEAD_DIM - 2 * ROT_HALF
    cs = jnp.concatenate([cos, cos, jnp.ones((TOKENS, rest), F32)], axis=1)
    sn = jnp.concatenate([-sin, sin, jnp.zeros((TOKENS, rest), F32)], axis=1)
    return cs, sn


def kernel(x, c, positions, norm_mix, norm_ffn, w_ada, b_ada, w_in, w_out, cmp_k_pos, cmp_k_w1, cmp_k_b1, cmp_k_w2, cmp_k_b2, cmp_v_pos, cmp_v_w1, cmp_v_b1, cmp_v_w2, cmp_v_b2, hgrn_norm, hgrn_lb, peer_wq, peer_keys, peer_u, peer_v, final_norm):
    xf = x.reshape(TOKENS, D_MODEL)
    rope_cs, rope_sn = _rope_tables(positions)
    c8 = jnp.concatenate([c, jnp.zeros((8 - BATCH, D_MODEL), F32)], axis=0)
    mod = ada_mod(c8, w_ada, b_ada).reshape(DEPTH * 8 * 6, 1, D_MODEL)
    lb_all = jnp.cumsum(jax.nn.softmax(hgrn_lb.astype(F32), axis=0), axis=0)
    lb_all = lb_all - lb_all[0:1]

    w_nsa, w_hgrn, w_gate = split_w_in(w_in)
    w_out_b = w_out.astype(BF16)
    cmp_pos = jnp.stack([cmp_k_pos, cmp_v_pos], axis=1)
    cmp_w1 = jnp.stack([cmp_k_w1, cmp_v_w1], axis=1).astype(BF16)
    cmp_b1 = jnp.stack([cmp_k_b1, cmp_v_b1], axis=1).reshape(DEPTH, 2, 1, CMP_HIDDEN)
    cmp_w2 = jnp.stack([cmp_k_w2, cmp_v_w2], axis=1).astype(BF16)
    cmp_b2 = jnp.stack([cmp_k_b2, cmp_v_b2], axis=1).reshape(DEPTH, 2, 1, HEAD_DIM)
    wq_b = peer_wq.astype(BF16)
    keys = peer_keys.reshape(DEPTH, PEER_HEADS * 2, PEER_KEYS, PEER_KEYS)
    u_b = peer_u.astype(BF16)
    v_b = peer_v.astype(BF16)

    h = adaln(xf, norm_mix[0], mod, 0)
    for l in range(DEPTH):
        p_nsa = in_proj(h, w_nsa, l)
        p_hgrn = in_proj(h, w_hgrn, l)
        p_gate = in_proj(h, w_gate, l)
        cmp_kv = compress(p_nsa, cmp_pos, cmp_w1, cmp_b1, cmp_w2, cmp_b2, l)
        o_nsa = nsa_attention(p_nsa, p_gate, cmp_kv, rope_cs, rope_sn)
        o_hgrn = hgrn(p_hgrn, lb_all[l], hgrn_norm[l])
        x1, ht = out_proj(o_nsa, o_hgrn, w_out_b, xf, norm_ffn[l], mod, l)
        eb, rb, ea, cnt = peer_scores(ht, wq_b, keys, l)
        out_t = peer_dense(ht, eb, rb, ea, cnt, u_b, v_b, l)
        if l + 1 < DEPTH:
            xf, h = peer_residual(x1, out_t, mod, l, norm_mix[l + 1])
        else:
            xf = final_residual_norm(x1, out_t, mod, l, final_norm)
    return xf.reshape(BATCH, SEQ, D_MODEL)
```

```python
import numpy as np
import jax
import jax.numpy as jnp
from jax import lax
from jax.experimental import pallas as pl
from jax.experimental.pallas import tpu as pltpu

F32 = jnp.float32
BF16 = jnp.bfloat16

D_MODEL = 2048
BATCH = 4
SEQ = 2048
TOKENS = BATCH * SEQ
DEPTH = 2
EPS = 1e-6

HEAD_DIM = 128
NSA_HEADS = 8
NSA_GROUPS = 2
NSA_HPG = NSA_HEADS // NSA_GROUPS
CMP_BLOCK = 32
CMP_STRIDE = 16
CMP_HIDDEN = 256
N_CMP = (SEQ - CMP_BLOCK) // CMP_STRIDE + 1
N_CMP_PAD = 128
SLC_BLOCK = 64
N_SLC = SEQ // SLC_BLOCK
SLC_TOPN = 16
N_LOCAL_FORCED = 2
WINDOW = 512
ROT_HALF = 16
ROPE_THETA = 500000.0
Q_TILE = 128
WIN_KEYS = WINDOW + Q_TILE

HGRN_HEADS = 8
HGRN_CHUNK = 64
HGRN_SUB = 16

PEER_HEADS = 8
PEER_KEYS = 128
PEER_EXPERTS = PEER_KEYS * PEER_KEYS
PEER_TOPK = 16

NSA_COLS = 2560
COL_Q, COL_KC, COL_VC, COL_KS, COL_VS, COL_KW, COL_VW = 0, 8, 10, 12, 14, 16, 18
COL_HQ, COL_HF, COL_HI, COL_HG = 0, 8, 16, 24

VMEM_LIMIT = 61 * 1024 * 1024


def _params(sem):
    return pltpu.CompilerParams(dimension_semantics=sem, vmem_limit_bytes=VMEM_LIMIT)


def _ada_kernel(c_ref, w_ref, b_ref, o_ref):
    c = c_ref[...]
    o_ref[0] = jnp.dot(c * jax.nn.sigmoid(c), w_ref[0], preferred_element_type=F32) + b_ref[0]


def ada_mod(c8, w_ada, b_ada):
    tn = 1024
    n = 6 * D_MODEL
    return pl.pallas_call(
        _ada_kernel,
        grid=(DEPTH, n // tn),
        in_specs=[
            pl.BlockSpec((8, D_MODEL), lambda l, j: (0, 0)),
            pl.BlockSpec((1, D_MODEL, tn), lambda l, j: (l, 0, j)),
            pl.BlockSpec((1, 1, tn), lambda l, j: (l, 0, j)),
        ],
        out_specs=pl.BlockSpec((1, 8, tn), lambda l, j: (l, 0, j)),
        out_shape=jax.ShapeDtypeStruct((DEPTH, 8, n), F32),
        compiler_params=_params(("parallel", "parallel")),
        name="ada_mod",
    )(c8, w_ada, b_ada.reshape(DEPTH, 1, n))


def _mod_spec(layer, k, rows_per_batch):
    return pl.BlockSpec((1, 1, D_MODEL), lambda i: ((layer * 8 + i // rows_per_batch) * 6 + k, 0, 0))


def _rms_mod(x, gain, scale, shift):
    y = x * lax.rsqrt(jnp.mean(x * x, axis=-1, keepdims=True) + EPS) * gain
    return y * (1.0 + scale) + shift


def _adaln_kernel(x_ref, gain_ref, sh_ref, sc_ref, h_ref):
    h_ref[...] = _rms_mod(x_ref[...], gain_ref[...], sc_ref[0], sh_ref[0]).astype(BF16)


def adaln(x, gain, mod, layer):
    tm = 256
    return pl.pallas_call(
        _adaln_kernel,
        grid=(TOKENS // tm,),
        in_specs=[
            pl.BlockSpec((tm, D_MODEL), lambda i: (i, 0)),
            pl.BlockSpec((1, D_MODEL), lambda i: (0, 0)),
            _mod_spec(layer, 0, SEQ // tm),
            _mod_spec(layer, 1, SEQ // tm),
        ],
        out_specs=pl.BlockSpec((tm, D_MODEL), lambda i: (i, 0)),
        out_shape=jax.ShapeDtypeStruct((TOKENS, D_MODEL), BF16),
        compiler_params=_params(("parallel",)),
        name="adaln",
    )(x, gain.reshape(1, D_MODEL), mod, mod)


def _matmul_nt_kernel(a_ref, bt_ref, o_ref):
    o_ref[...] = lax.dot_general(a_ref[...], bt_ref[...], (((1,), (1,)), ((), ())), preferred_element_type=F32)


def in_proj(h, wt, layer):
    n = wt.shape[1]
    tm = 1024
    tn = next(t for t in (1280, 1024, 256) if n % t == 0)
    return pl.pallas_call(
        _matmul_nt_kernel,
        grid=(TOKENS // tm, n // tn),
        in_specs=[
            pl.BlockSpec((tm, D_MODEL), lambda i, j: (i, 0)),
            pl.BlockSpec((None, tn, D_MODEL), lambda i, j: (layer, j, 0)),
        ],
        out_specs=pl.BlockSpec((tm, tn), lambda i, j: (i, j)),
        out_shape=jax.ShapeDtypeStruct((TOKENS, n), F32),
        compiler_params=_params(("parallel", "parallel")),
        name="in_proj",
    )(h, wt)


def split_w_in(w):
    wt = jnp.swapaxes(w, 1, 2)
    gl = wt[:, NSA_COLS:NSA_COLS + 24]
    z = jnp.zeros((w.shape[0], 116, D_MODEL), w.dtype)
    gates = jnp.concatenate([gl[:, :12], z, gl[:, 12:], z], axis=1)
    return wt[:, :NSA_COLS].astype(BF16), wt[:, NSA_COLS + 24:].astype(BF16), gates.astype(BF16)


def _compress_kernel(t_ref, pos_ref, w1_ref, b1_ref, w2_ref, b2_ref, o_ref, pad_ref):
    pad_ref[0:SEQ, :] = t_ref[...]
    pad_ref[SEQ:SEQ + CMP_BLOCK, :] = jnp.zeros((CMP_BLOCK, HEAD_DIM), F32)
    acc = jnp.zeros((N_CMP_PAD, CMP_HIDDEN), F32)
    for l in range(CMP_BLOCK):
        tl = pad_ref[pl.ds(l, N_CMP_PAD, stride=CMP_STRIDE), :] + pos_ref[0, l:l + 1, :]
        acc = acc + jnp.dot(tl.astype(BF16), w1_ref[0, l * HEAD_DIM:(l + 1) * HEAD_DIM, :],
                            preferred_element_type=F32)
    hid = jax.nn.gelu(acc + b1_ref[0])
    o_ref[0, 0] = jnp.dot(hid.astype(BF16), w2_ref[0], preferred_element_type=F32) + b2_ref[0]


def compress(proj, pos, w1, b1, w2, b2, layer):
    return pl.pallas_call(
        _compress_kernel,
        grid=(2, BATCH, NSA_GROUPS),
        in_specs=[
            pl.BlockSpec((SEQ, HEAD_DIM), lambda kv, b, g: (b, COL_KC + 2 * kv + g)),
            pl.BlockSpec((None, 1, CMP_BLOCK, HEAD_DIM), lambda kv, b, g: (layer, kv, 0, 0)),
            pl.BlockSpec((None, 1, CMP_BLOCK * HEAD_DIM, CMP_HIDDEN), lambda kv, b, g: (layer, kv, 0, 0)),
            pl.BlockSpec((None, 1, 1, CMP_HIDDEN), lambda kv, b, g: (layer, kv, 0, 0)),
            pl.BlockSpec((None, 1, CMP_HIDDEN, HEAD_DIM), lambda kv, b, g: (layer, kv, 0, 0)),
            pl.BlockSpec((None, 1, 1, HEAD_DIM), lambda kv, b, g: (layer, kv, 0, 0)),
        ],
        out_specs=pl.BlockSpec((1, 1, N_CMP_PAD, HEAD_DIM), lambda kv, b, g: (kv, b * NSA_GROUPS + g, 0, 0)),
        out_shape=jax.ShapeDtypeStruct((2, BATCH * NSA_GROUPS, N_CMP_PAD, HEAD_DIM), F32),
        scratch_shapes=[pltpu.VMEM((SEQ + CMP_BLOCK, HEAD_DIM), F32)],
        compiler_params=_params(("arbitrary", "arbitrary", "arbitrary")),
        name="nsa_compress",
    )(proj, pos, w1, b1, w2, b2)


def _rope(t, cs, sn):
    lane = lax.broadcasted_iota(jnp.int32, t.shape, 1)
    swapped = jnp.where(lane < ROT_HALF, pltpu.roll(t, HEAD_DIM - ROT_HALF, axis=1), pltpu.roll(t, ROT_HALF, axis=1))
    return t * cs + swapped * sn


def _dot_nt(a, b):
    return lax.dot_general(a, b, (((1,), (1,)), ((), ())), preferred_element_type=F32)


MASKED = -1e30
GROUP_TILES = 2


def _nsa_kernel(q_ref, kc_ref, vc_ref, ks_ref, vs_ref, kw_ref, vw_ref, gl_ref, cs_ref, sn_ref, wov_ref,
                o_ref, kaug, vaug, kwr, vwaug):
    scale = HEAD_DIM ** -0.5
    neg_inf = -jnp.inf
    row = lax.broadcasted_iota(jnp.int32, (SEQ, HEAD_DIM), 0)
    lane = lax.broadcasted_iota(jnp.int32, (SEQ, HEAD_DIM), 1)
    kaug[:, 0:HEAD_DIM] = _rope(ks_ref[...], cs_ref[...], sn_ref[...]).astype(BF16)
    kaug[:, HEAD_DIM:2 * HEAD_DIM] = jnp.where(lane == (row >> 6), 1.0, 0.0).astype(BF16)
    kwr[...] = _rope(kw_ref[...], cs_ref[...], sn_ref[...]).astype(BF16)
    ones = jnp.ones((SEQ, HEAD_DIM), BF16)
    vaug[:, 0:HEAD_DIM] = vs_ref[...].astype(BF16)
    vaug[:, HEAD_DIM:2 * HEAD_DIM] = ones
    vwaug[:, 0:HEAD_DIM] = vw_ref[...].astype(BF16)
    vwaug[:, HEAD_DIM:2 * HEAD_DIM] = ones
    kcb = kc_ref[0, 0].astype(BF16)
    vcb = vc_ref[0, 0].astype(BF16)

    def tile(i, extent):
        t0 = pl.multiple_of(i * Q_TILE, Q_TILE)
        q4 = q_ref[pl.ds(t0, Q_TILE), :] * scale
        cs = cs_ref[pl.ds(t0, Q_TILE), :]
        sn = sn_ref[pl.ds(t0, Q_TILE), :]
        gate = jax.nn.sigmoid(gl_ref[pl.ds(t0, Q_TILE), :])
        qh = [q4[:, h * HEAD_DIM:(h + 1) * HEAD_DIM] for h in range(NSA_HPG)]

        qs = jnp.concatenate(qh, axis=0).astype(BF16)
        sc = _dot_nt(qs, kcb)
        rows = lax.broadcasted_iota(jnp.int32, sc.shape, 0)
        ncol = lax.broadcasted_iota(jnp.int32, sc.shape, 1)
        tok = t0 + (rows & (Q_TILE - 1))
        valid = ncol * CMP_STRIDE + (CMP_BLOCK - 1) <= tok
        s = jnp.where(valid, sc, neg_inf)
        m = jnp.max(s, axis=-1, keepdims=True)
        m = jnp.where(m == neg_inf, 0.0, m)
        e = jnp.where(valid, jnp.exp(s - m), 0.0)
        p = e / jnp.maximum(jnp.sum(e, axis=-1, keepdims=True), 1e-30)
        o_cmp = jnp.dot(p.astype(BF16), vcb, preferred_element_type=F32)

        p4 = p[0:Q_TILE] + p[Q_TILE:2 * Q_TILE] + p[2 * Q_TILE:3 * Q_TILE] + p[3 * Q_TILE:4 * Q_TILE]
        imp = jnp.dot(p4, wov_ref[...], preferred_element_type=F32)
        jj = lax.broadcasted_iota(jnp.int32, imp.shape, 1)
        tt = t0 + lax.broadcasted_iota(jnp.int32, imp.shape, 0)
        back = (tt >> 6) - jj
        forced = (jj == 0) | (back.astype(jnp.uint32) < N_LOCAL_FORCED)
        score = jnp.where(forced, jnp.inf, jnp.where(jj * SLC_BLOCK <= tt, imp, neg_inf))
        score_t = score.T[0:N_SLC]
        jrow = lax.broadcasted_iota(jnp.int32, score_t.shape, 0)
        rank = jnp.zeros(score_t.shape, F32)
        for jp in range(N_SLC):
            other = score_t[jp:jp + 1]
            later = jnp.where(jrow > jp, 1.0, 0.0)
            rank = rank + jnp.where(other > score_t, 1.0, jnp.where(other == score_t, later, 0.0))
        bias_t = jnp.where(rank < SLC_TOPN, 0.0, MASKED)
        sel_bias = jnp.concatenate([bias_t, jnp.zeros((HEAD_DIM - N_SLC, Q_TILE), F32)], axis=0).T

        tail = GROUP_TILES * Q_TILE
        rows4 = NSA_HPG * Q_TILE
        kpos = (extent - tail) + lax.broadcasted_iota(jnp.int32, (rows4, tail), 1)
        ok_tail = kpos <= t0 + (lax.broadcasted_iota(jnp.int32, (rows4, tail), 0) & (Q_TILE - 1))

        qr = jnp.concatenate([_rope(qh[h], cs, sn) for h in range(NSA_HPG)], axis=0)
        qa = jnp.concatenate([qr, jnp.concatenate([sel_bias] * NSA_HPG, axis=0)], axis=1).astype(BF16)
        s1 = _dot_nt(qa, kaug[0:extent, :])
        s1_tail = jnp.where(ok_tail, s1[:, extent - tail:], MASKED)
        s1 = s1_tail if extent == tail else jnp.concatenate([s1[:, :extent - tail], s1_tail], axis=1)
        e1 = jnp.exp(s1 - jnp.max(s1, axis=-1, keepdims=True)).astype(BF16)
        r1 = jnp.dot(e1, vaug[0:extent, :], preferred_element_type=F32)
        o_slc = r1[:, :HEAD_DIM] / r1[:, HEAD_DIM:HEAD_DIM + 1]

        ws = pl.multiple_of(jnp.clip(t0 - WINDOW, 0, SEQ - WIN_KEYS), Q_TILE)
        wk = ws + lax.broadcasted_iota(jnp.int32, (rows4, WIN_KEYS), 1)
        wt = t0 + (lax.broadcasted_iota(jnp.int32, (rows4, WIN_KEYS), 0) & (Q_TILE - 1))
        ok_win = (wt - wk).astype(jnp.uint32) < WINDOW
        s2 = jnp.where(ok_win, _dot_nt(qr.astype(BF16), kwr[pl.ds(ws, WIN_KEYS), :]), MASKED)
        e2 = jnp.exp(s2 - jnp.max(s2, axis=-1, keepdims=True)).astype(BF16)
        r2 = jnp.dot(e2, vwaug[pl.ds(ws, WIN_KEYS), :], preferred_element_type=F32)
        o_win = r2[:, :HEAD_DIM] / r2[:, HEAD_DIM:HEAD_DIM + 1]

        outs = []
        for h in range(NSA_HPG):
            hr = slice(h * Q_TILE, (h + 1) * Q_TILE)
            outs.append(gate[:, 3 * h:3 * h + 1] * o_cmp[hr] + gate[:, 3 * h + 1:3 * h + 2] * o_slc[hr]
                        + gate[:, 3 * h + 2:3 * h + 3] * o_win[hr])
        o_ref[pl.ds(t0, Q_TILE), :] = jnp.concatenate(outs, axis=1).astype(BF16)

    for grp in range(SEQ // (GROUP_TILES * Q_TILE)):
        extent = (grp + 1) * GROUP_TILES * Q_TILE

        def body(ii, carry, grp=grp, extent=extent):
            tile(grp * GROUP_TILES + ii, extent)
            return carry

        lax.fori_loop(0, GROUP_TILES, body, 0)


def _cmp_to_slc_weights():
    cs = np.arange(N_CMP) * CMP_STRIDE
    ce = cs + CMP_BLOCK
    ss = np.arange(N_SLC) * SLC_BLOCK
    se = ss + SLC_BLOCK
    ov = np.clip(np.minimum(ce[:, None], se[None, :]) - np.maximum(cs[:, None], ss[None, :]), 0, None)
    w = np.zeros((N_CMP_PAD, HEAD_DIM), np.float32)
    w[:N_CMP, :N_SLC] = ov / CMP_BLOCK
    return w


def nsa_attention(proj, gates, cmp_kv, rope_cs, rope_sn):
    gw = NSA_HPG * HEAD_DIM
    col = lambda c: pl.BlockSpec((SEQ, HEAD_DIM), lambda b, g: (b, c + g))
    return pl.pallas_call(
        _nsa_kernel,
        grid=(BATCH, NSA_GROUPS),
        in_specs=[
            pl.BlockSpec((SEQ, gw), lambda b, g: (b, g)),
            pl.BlockSpec((1, 1, N_CMP_PAD, HEAD_DIM), lambda b, g: (0, b * NSA_GROUPS + g, 0, 0)),
            pl.BlockSpec((1, 1, N_CMP_PAD, HEAD_DIM), lambda b, g: (1, b * NSA_GROUPS + g, 0, 0)),
            col(COL_KS), col(COL_VS), col(COL_KW), col(COL_VW), col(0),
            pl.BlockSpec((SEQ, HEAD_DIM), lambda b, g: (b, 0)),
            pl.BlockSpec((SEQ, HEAD_DIM), lambda b, g: (b, 0)),
            pl.BlockSpec((N_CMP_PAD, HEAD_DIM), lambda b, g: (0, 0)),
        ],
        out_specs=pl.BlockSpec((SEQ, gw), lambda b, g: (b, g)),
        out_shape=jax.ShapeDtypeStruct((TOKENS, NSA_HEADS * HEAD_DIM), BF16),
        scratch_shapes=[pltpu.VMEM((SEQ, 2 * HEAD_DIM), BF16), pltpu.VMEM((SEQ, 2 * HEAD_DIM), BF16),
                        pltpu.VMEM((SEQ, HEAD_DIM), BF16), pltpu.VMEM((SEQ, 2 * HEAD_DIM), BF16)],
        compiler_params=_params(("parallel", "parallel")),
        name="nsa_attention",
    )(proj, cmp_kv, cmp_kv, proj, proj, proj, proj, gates, rope_cs, rope_sn, jnp.asarray(_cmp_to_slc_weights()))


HGRN_HEADS_PER_STEP = 4


def _hgrn_chunk(qc, fc, vc, gc, lb, log_lb, log_1m, gn, tri, state_t):
    C, SUB = HGRN_CHUNK, HGRN_SUB
    row8 = lax.broadcasted_iota(jnp.int32, (8, HEAD_DIM), 0)
    qh = qc * jax.nn.sigmoid(qc)
    log_sig = jnp.minimum(fc, 0.0) - jnp.log(1.0 + jnp.exp(-jnp.abs(fc)))
    b = log_1m + log_sig
    lf = jnp.maximum(log_lb, b) + jnp.log(1.0 + jnp.exp(-jnp.abs(log_lb - b)))
    kh = (1.0 - lb) * jax.nn.sigmoid(-fc)
    a = jnp.dot(tri, lf, preferred_element_type=F32)
    inter = _dot_nt(qh * jnp.exp(a), state_t)
    parts = []
    for blk in range(C // SUB):
        lo = blk * SUB
        ab, qb, kb, vb = a[lo:lo + SUB], qh[lo:lo + SUB], kh[lo:lo + SUB], vc[lo:lo + SUB]
        acc = inter[lo:lo + SUB]
        if blk > 0:
            a_ref_row = a[lo - 1:lo]
            qt = qb * jnp.exp(ab - a_ref_row)
            kt = kh[0:lo] * jnp.exp(a_ref_row - a[0:lo])
            acc = acc + jnp.dot(_dot_nt(qt, kt), vc[0:lo], preferred_element_type=F32)
        for r0 in range(0, SUB, 8):
            ar, qr, acc_r = ab[r0:r0 + 8], qb[r0:r0 + 8], acc[r0:r0 + 8]
            for s in range(r0 + 8):
                rel = ar - ab[s:s + 1]
                if s >= r0:
                    rel = jnp.where(row8 >= s - r0, rel, -jnp.inf)
                w = jnp.sum(qr * kb[s:s + 1] * jnp.exp(rel), axis=-1, keepdims=True)
                acc_r = acc_r + w * vb[s:s + 1]
            parts.append(acc_r)
    o = jnp.concatenate(parts, axis=0)
    a_end = a[C - 1:C]
    kt_end = kh * jnp.exp(a_end - a)
    state_t = state_t * jnp.exp(a_end) + lax.dot_general(vc, kt_end, (((0,), (0,)), ((), ())),
                                                          preferred_element_type=F32)
    o = o * lax.rsqrt(jnp.mean(o * o, axis=-1, keepdims=True) + EPS) * gn
    return (o * jax.nn.sigmoid(gc)).astype(BF16), state_t


def _hgrn_kernel(q_ref, f_ref, i_ref, g_ref, lb_ref, gn_ref, tri_ref, o_ref):
    C, HP = HGRN_CHUNK, HGRN_HEADS_PER_STEP
    lbs = [lb_ref[hp] for hp in range(HP)]
    log_lbs = [jnp.log(lb) for lb in lbs]
    log_1ms = [jnp.log(1.0 - lb) for lb in lbs]
    gns = [gn_ref[hp] for hp in range(HP)]

    def chunk(c, states):
        r0 = pl.multiple_of(c * C, C)
        qc = q_ref[pl.ds(r0, C), :]
        fc = f_ref[pl.ds(r0, C), :]
        vc = i_ref[pl.ds(r0, C), :]
        gc = g_ref[pl.ds(r0, C), :]
        outs, new_states = [], []
        for hp in range(HP):
            cols = slice(hp * HEAD_DIM, (hp + 1) * HEAD_DIM)
            o, st = _hgrn_chunk(qc[:, cols], fc[:, cols], vc[:, cols], gc[:, cols], lbs[hp], log_lbs[hp], log_1ms[hp],
                                gns[hp], tri_ref[...], states[hp])
            outs.append(o)
            new_states.append(st)
        o_ref[pl.ds(r0, C), :] = jnp.concatenate(outs, axis=1)
        return tuple(new_states)

    lax.fori_loop(0, SEQ // C, chunk, tuple(jnp.zeros((HEAD_DIM, HEAD_DIM), F32) for _ in range(HP)), unroll=4)


def hgrn(proj, lb, gain):
    hp = HGRN_HEADS_PER_STEP
    col = lambda c: pl.BlockSpec((SEQ, hp * HEAD_DIM), lambda b, h: (b, c // hp + h))
    vec = pl.BlockSpec((hp, 1, HEAD_DIM), lambda b, h: (h, 0, 0))
    tri = np.tril(np.ones((HGRN_CHUNK, HGRN_CHUNK), np.float32))
    return pl.pallas_call(
        _hgrn_kernel,
        grid=(BATCH, HGRN_HEADS // hp),
        in_specs=[col(COL_HQ), col(COL_HF), col(COL_HI), col(COL_HG), vec, vec,
                  pl.BlockSpec((HGRN_CHUNK, HGRN_CHUNK), lambda b, h: (0, 0))],
        out_specs=pl.BlockSpec((SEQ, hp * HEAD_DIM), lambda b, h: (b, h)),
        out_shape=jax.ShapeDtypeStruct((TOKENS, HGRN_HEADS * HEAD_DIM), BF16),
        compiler_params=_params(("parallel", "parallel")),
        name="hgrn2",
    )(proj, proj, proj, proj, lb.reshape(HGRN_HEADS, 1, HEAD_DIM), gain.reshape(HGRN_HEADS, 1, HEAD_DIM),
      jnp.asarray(tri))


def _out_proj_kernel(on_ref, oh_ref, wn_ref, wh_ref, x_ref, g_ref, gain_ref, sh_ref, sc_ref, x1_ref, ht_ref):
    mix = (jnp.dot(on_ref[...], wn_ref[...], preferred_element_type=F32)
           + jnp.dot(oh_ref[...], wh_ref[...], preferred_element_type=F32))
    x1 = x_ref[...] + g_ref[0] * mix
    x1_ref[...] = x1
    ht_ref[...] = _rms_mod(x1, gain_ref[...], sc_ref[0], sh_ref[0]).T.astype(BF16)


def out_proj(o_nsa, o_hgrn, w_out, x, gain, mod, layer):
    tm = 256
    half = D_MODEL // 2
    rows = pl.BlockSpec((tm, half), lambda i: (i, 0))
    full = pl.BlockSpec((tm, D_MODEL), lambda i: (i, 0))
    return pl.pallas_call(
        _out_proj_kernel,
        grid=(TOKENS // tm,),
        in_specs=[rows, rows,
                  pl.BlockSpec((None, half, D_MODEL), lambda i: (layer, 0, 0)),
                  pl.BlockSpec((None, half, D_MODEL), lambda i: (layer, 1, 0)),
                  full, _mod_spec(layer, 2, SEQ // tm),
                  pl.BlockSpec((1, D_MODEL), lambda i: (0, 0)),
                  _mod_spec(layer, 3, SEQ // tm), _mod_spec(layer, 4, SEQ // tm)],
        out_specs=[full, pl.BlockSpec((D_MODEL, tm), lambda i: (0, i))],
        out_shape=[jax.ShapeDtypeStruct((TOKENS, D_MODEL), F32), jax.ShapeDtypeStruct((D_MODEL, TOKENS), BF16)],
        compiler_params=_params(("parallel",)),
        name="out_proj",
    )(o_nsa, o_hgrn, w_out, w_out, x, mod, gain.reshape(1, D_MODEL), mod, mod)


NOT_TOP = 127.0


def _pack_pairs(x):
    return pltpu.bitcast(x, jnp.uint32)


def _unpack_pairs(x):
    return pltpu.bitcast(x, BF16)


def _top_values(x, k, want_rank=False):
    vals = []
    rank = jnp.full(x.shape, NOT_TOP, F32) if want_rank else None
    for i in range(k):
        m = jnp.max(x, axis=0, keepdims=True)
        vals.append(m)
        hit = x == m
        if want_rank:
            rank = jnp.where(hit, float(i), rank)
        x = jnp.where(hit, -jnp.inf, x)
    return vals, rank


def _dot_tn(a, b):
    return lax.dot_general(a, b, (((0,), (0,)), ((), ())), preferred_element_type=F32)


def _peer_score_kernel(ht_ref, wq_ref, keys_ref, eb_ref, rb_ref, ea_ref, cnt_ref):
    k = PEER_TOPK
    qt = _dot_tn(wq_ref[...], ht_ref[...])
    for h in range(PEER_HEADS):
        sa = jnp.dot(keys_ref[2 * h], qt[(2 * h) * PEER_KEYS:(2 * h + 1) * PEER_KEYS], preferred_element_type=F32)
        sb = jnp.dot(keys_ref[2 * h + 1], qt[(2 * h + 1) * PEER_KEYS:(2 * h + 2) * PEER_KEYS],
                     preferred_element_type=F32)
        top_a, _ = _top_values(sa, k)
        top_b, rank_b = _top_values(sb, k, want_rank=True)
        a16 = jnp.concatenate(top_a, axis=0)
        b16 = jnp.concatenate(top_b, axis=0)
        cand = jnp.concatenate([a16 + top_b[0]] + [a16[0:8] + top_b[j] for j in range(1, 8)]
                               + [top_a[0] + b16[8:16]], axis=0)
        best, _ = _top_values(cand, k)
        z = jnp.zeros_like(best[0])
        for v in best:
            z = z + jnp.exp(v - best[0])
        cnt = jnp.zeros_like(sa)
        for j in range(k // 2):
            cnt = cnt + jnp.where(sa + top_b[j] >= best[k - 1], 1.0, 0.0)
        extra = jnp.zeros_like(best[0])
        for j in range(k // 2, k):
            extra = extra + jnp.where(top_a[0] + top_b[j] >= best[k - 1], 1.0, 0.0)
        cnt = cnt + jnp.where(sa == top_a[0], extra, 0.0)
        half = slice(h * PEER_KEYS // 2, (h + 1) * PEER_KEYS // 2)
        eb_ref[half, :] = _pack_pairs(jnp.exp(sb - top_b[0]).astype(BF16))
        rb_ref[half, :] = _pack_pairs(rank_b.astype(BF16))
        ea_ref[h] = jnp.exp(sa - top_a[0]) / z
        cnt_ref[h] = cnt


def peer_scores(ht, wq, keys, layer):
    tm = 256
    rows = PEER_HEADS * PEER_KEYS
    spec2 = pl.BlockSpec((rows // 2, tm), lambda i: (0, i))
    shape2 = jax.ShapeDtypeStruct((rows // 2, TOKENS), jnp.uint32)
    spec3 = pl.BlockSpec((PEER_HEADS, PEER_KEYS, tm), lambda i: (0, 0, i))
    shape3 = jax.ShapeDtypeStruct((PEER_HEADS, PEER_KEYS, TOKENS), F32)
    return pl.pallas_call(
        _peer_score_kernel,
        grid=(TOKENS // tm,),
        in_specs=[pl.BlockSpec((D_MODEL, tm), lambda i: (0, i)),
                  pl.BlockSpec((None, D_MODEL, 2 * rows), lambda i: (layer, 0, 0)),
                  pl.BlockSpec((None, PEER_HEADS * 2, PEER_KEYS, PEER_KEYS), lambda i: (layer, 0, 0, 0))],
        out_specs=[spec2, spec2, spec3, spec3],
        out_shape=[shape2, shape2, shape3, shape3],
        compiler_params=_params(("parallel",)),
        name="peer_scores",
    )(ht, wq, keys)


PEER_TM = 1024
PEER_TE = 1024
PEER_LANES = 128


def _gelu_to_bf16(x):
    c = float(np.sqrt(2.0 / np.pi))
    t = jnp.tanh((x * (c + (c * 0.044715) * (x * x))).astype(BF16))
    return (0.5 * x.astype(BF16)) * (1.0 + t)


def _peer_dense_kernel(xt_ref, eb_ref, rb_ref, ea_ref, cnt_ref, u_ref, v_ref, o_ref, gw_ref):
    @pl.when(pl.program_id(1) == 0)
    def _():
        o_ref[...] = jnp.zeros_like(o_ref)

    half = PEER_KEYS // 2

    def rows_bf16(ref, h, ab, cols):
        tile16 = jnp.broadcast_to(ref[h, ab:ab + 1, cols], (16, PEER_LANES)).astype(BF16)
        return jnp.concatenate([tile16] * (half // 16), axis=0)

    act = jnp.dot(u_ref[...], xt_ref[...], preferred_element_type=F32)
    for ab in range(PEER_TE // PEER_KEYS):
        for part in range(2):
            rows = slice(ab * PEER_KEYS + part * half, ab * PEER_KEYS + (part + 1) * half)
            for lt in range(PEER_TM // PEER_LANES):
                cols = slice(lt * PEER_LANES, (lt + 1) * PEER_LANES)
                w = jnp.zeros((half, PEER_LANES), BF16)
                for h in range(PEER_HEADS):
                    hb = slice((2 * h + part) * half // 2, (2 * h + part + 1) * half // 2)
                    eb = _unpack_pairs(eb_ref[hb, cols])
                    rb = _unpack_pairs(rb_ref[hb, cols])
                    cnt = rows_bf16(cnt_ref, h, ab, cols)
                    w = w + jnp.where(rb < cnt, eb * rows_bf16(ea_ref, h, ab, cols), 0.0)
                gw_ref[rows, cols] = _gelu_to_bf16(act[rows, cols]) * w
    o_ref[...] += _dot_tn(v_ref[...], gw_ref[...])


def peer_dense(ht, eb, rb, ea, cnt, u, v, layer):
    ab = PEER_TE // PEER_KEYS
    rows = pl.BlockSpec((PEER_HEADS, ab, PEER_TM), lambda i, j: (0, j, i))
    table = pl.BlockSpec((PEER_HEADS * PEER_KEYS // 2, PEER_TM), lambda i, j: (0, i))
    return pl.pallas_call(
        _peer_dense_kernel,
        grid=(TOKENS // PEER_TM, PEER_EXPERTS // PEER_TE),
        in_specs=[pl.BlockSpec((D_MODEL, PEER_TM), lambda i, j: (0, i)),
                  table, table, rows, rows,
                  pl.BlockSpec((None, PEER_TE, D_MODEL), lambda i, j: (layer, j, 0)),
                  pl.BlockSpec((None, PEER_TE, D_MODEL), lambda i, j: (layer, j, 0))],
        out_specs=pl.BlockSpec((D_MODEL, PEER_TM), lambda i, j: (0, i)),
        out_shape=jax.ShapeDtypeStruct((D_MODEL, TOKENS), F32),
        scratch_shapes=[pltpu.VMEM((PEER_TE, PEER_TM), BF16)],
        compiler_params=_params(("parallel", "arbitrary")),
        name="peer_dense",
    )(ht, eb, rb, ea, cnt, u, v)


def _peer_resid_kernel(x_ref, ot_ref, g_ref, gain_ref, sh_ref, sc_ref, x2_ref, h_ref):
    x2 = x_ref[...] + g_ref[0] * ot_ref[...].T
    x2_ref[...] = x2
    h_ref[...] = _rms_mod(x2, gain_ref[...], sc_ref[0], sh_ref[0]).astype(BF16)


def _final_kernel(x_ref, ot_ref, g_ref, gain_ref, o_ref):
    x2 = x_ref[...] + g_ref[0] * ot_ref[...].T
    o_ref[...] = x2 * lax.rsqrt(jnp.mean(x2 * x2, axis=-1, keepdims=True) + EPS) * gain_ref[...]


def peer_residual(x1, out_t, mod, layer, gain_next):
    tm = 256
    full = pl.BlockSpec((tm, D_MODEL), lambda i: (i, 0))
    return pl.pallas_call(
        _peer_resid_kernel,
        grid=(TOKENS // tm,),
        in_specs=[full, pl.BlockSpec((D_MODEL, tm), lambda i: (0, i)), _mod_spec(layer, 5, SEQ // tm),
                  pl.BlockSpec((1, D_MODEL), lambda i: (0, 0)),
                  _mod_spec(layer + 1, 0, SEQ // tm), _mod_spec(layer + 1, 1, SEQ // tm)],
        out_specs=[full, full],
        out_shape=[jax.ShapeDtypeStruct((TOKENS, D_MODEL), F32), jax.ShapeDtypeStruct((TOKENS, D_MODEL), BF16)],
        compiler_params=_params(("parallel",)),
        name="peer_residual",
    )(x1, out_t, mod, gain_next.reshape(1, D_MODEL), mod, mod)


def final_residual_norm(x1, out_t, mod, layer, gain):
    tm = 256
    full = pl.BlockSpec((tm, D_MODEL), lambda i: (i, 0))
    return pl.pallas_call(
        _final_kernel,
        grid=(TOKENS // tm,),
        in_specs=[full, pl.BlockSpec((D_MODEL, tm), lambda i: (0, i)), _mod_spec(layer, 5, SEQ // tm),
                  pl.BlockSpec((1, D_MODEL), lambda i: (0, 0))],
        out_specs=full,
        out_shape=jax.ShapeDtypeStruct((TOKENS, D_MODEL), F32),
        compiler_params=_params(("parallel",)),
        name="final_norm",
    )(x1, out_t, mod, gain.reshape(1, D_MODEL))


def _rope_tables(positions):
    inv = ROPE_THETA ** (-jnp.arange(0, 2 * ROT_HALF, 2, dtype=F32) / (2 * ROT_HALF))
    ang = positions.astype(F32).reshape(TOKENS, 1) * inv
    cos, sin = jnp.cos(ang), jnp.sin(ang)
    rest = HEAD_DIM - 2 * ROT_HALF
    cs = jnp.concatenate([cos, cos, jnp.ones((TOKENS, rest), F32)], axis=1)
    sn = jnp.concatenate([-sin, sin, jnp.zeros((TOKENS, rest), F32)], axis=1)
    return cs, sn


def kernel(x, c, positions, norm_mix, norm_ffn, w_ada, b_ada, w_in, w_out, cmp_k_pos, cmp_k_w1, cmp_k_b1, cmp_k_w2, cmp_k_b2, cmp_v_pos, cmp_v_w1, cmp_v_b1, cmp_v_w2, cmp_v_b2, hgrn_norm, hgrn_lb, peer_wq, peer_keys, peer_u, peer_v, final_norm):
    xf = x.reshape(TOKENS, D_MODEL)
    rope_cs, rope_sn = _rope_tables(positions)
    c8 = jnp.concatenate([c, jnp.zeros((8 - BATCH, D_MODEL), F32)], axis=0)
    mod = ada_mod(c8, w_ada, b_ada).reshape(DEPTH * 8 * 6, 1, D_MODEL)
    lb_all = jnp.cumsum(jax.nn.softmax(hgrn_lb.astype(F32), axis=0), axis=0)
    lb_all = lb_all - lb_all[0:1]

    w_nsa, w_hgrn, w_gate = split_w_in(w_in)
    w_out_b = w_out.astype(BF16)
    cmp_pos = jnp.stack([cmp_k_pos, cmp_v_pos], axis=1)
    cmp_w1 = jnp.stack([cmp_k_w1, cmp_v_w1], axis=1).astype(BF16)
    cmp_b1 = jnp.stack([cmp_k_b1, cmp_v_b1], axis=1).reshape(DEPTH, 2, 1, CMP_HIDDEN)
    cmp_w2 = jnp.stack([cmp_k_w2, cmp_v_w2], axis=1).astype(BF16)
    cmp_b2 = jnp.stack([cmp_k_b2, cmp_v_b2], axis=1).reshape(DEPTH, 2, 1, HEAD_DIM)
    wq_b = peer_wq.astype(BF16)
    keys = peer_keys.reshape(DEPTH, PEER_HEADS * 2, PEER_KEYS, PEER_KEYS)
    u_b = peer_u.astype(BF16)
    v_b = peer_v.astype(BF16)

    h = adaln(xf, norm_mix[0], mod, 0)
    for l in range(DEPTH):
        p_nsa = in_proj(h, w_nsa, l)
        p_hgrn = in_proj(h, w_hgrn, l)
        p_gate = in_proj(h, w_gate, l)
        cmp_kv = compress(p_nsa, cmp_pos, cmp_w1, cmp_b1, cmp_w2, cmp_b2, l)
        o_nsa = nsa_attention(p_nsa, p_gate, cmp_kv, rope_cs, rope_sn)
        o_hgrn = hgrn(p_hgrn, lb_all[l], hgrn_norm[l])
        x1, ht = out_proj(o_nsa, o_hgrn, w_out_b, xf, norm_ffn[l], mod, l)
        eb, rb, ea, cnt = peer_scores(ht, wq_b, keys, l)
        out_t = peer_dense(ht, eb, rb, ea, cnt, u_b, v_b, l)
        if l + 1 < DEPTH:
            xf, h = peer_residual(x1, out_t, mod, l, norm_mix[l + 1])
        else:
            xf = final_residual_norm(x1, out_t, mod, l, final_norm)
    return xf.reshape(BATCH, SEQ, D_MODEL)
```
